```python
import math
import jax, jax.numpy as jnp
from jax import lax
import numpy as np

D_MODEL = 1024
BATCH = 8
SEQ = 4096
DEPTH = 4

GRID_W = 64
CTX_LEN = 256
N_AB = (DEPTH + 1) // 2
N_C = DEPTH // 2
A_HEADS = 8
A_KV_HEADS = 2
A_HEAD_DIM = 64
A_WIDTH = A_HEADS * A_HEAD_DIM
A_KV_WIDTH = A_KV_HEADS * A_HEAD_DIM
WINDOW = 128
Q_BLOCK = WINDOW
ROPE_BASE = 10000.0
B_WIDTH = D_MODEL // 2
CONV_W = 3
AB_IN_WIDTH = 2 * A_WIDTH + 2 * A_KV_WIDTH + 4 * B_WIDTH
AB_OUT_WIDTH = A_WIDTH + B_WIDTH
C_HEADS = 8
C_KEY_DIM = 128
C_VAL_DIM = D_MODEL // C_HEADS
C_F_WIDTH = C_HEADS * C_KEY_DIM
C_I_WIDTH = C_HEADS * C_VAL_DIM
C_IN_WIDTH = 3 * C_F_WIDTH + 2 * C_I_WIDTH
CHUNK = 32
DEEPNORM_ALPHA = (2 * DEPTH) ** 0.25
DEEPNORM_BETA = (8 * DEPTH) ** -0.25
LN_EPS = 1e-5
RMS_EPS = 1e-6

kernel_name = "hybrid_swa_shortconv_hgrn2_dit_block"


def split_cols(p, widths):
    idx = [int(v) for v in np.cumsum(widths)[:-1]]
    return jnp.split(p, idx, axis=-1)


def layer_norm(x, g, b):
    xf = x.astype(jnp.float32)
    mu = xf.mean(-1, keepdims=True)
    var = jnp.square(xf - mu).mean(-1, keepdims=True)
    return ((xf - mu) * lax.rsqrt(var + LN_EPS) * g + b).astype(x.dtype)


def modulate(h, shift, scale):
    return h * (1 + scale) + shift


def rope_1d(x, pos):
    half = x.shape[-1] // 2
    freqs = ROPE_BASE ** (-jnp.arange(half, dtype=jnp.float32) / half)
    ang = pos.astype(jnp.float32)[:, None] * freqs[None, :]
    cos = jnp.cos(ang)[:, None, :].astype(x.dtype)
    sin = jnp.sin(ang)[:, None, :].astype(x.dtype)
    x1, x2 = x[..., :half], x[..., half:]
    return jnp.concatenate([x1 * cos - x2 * sin, x1 * sin + x2 * cos], axis=-1)


def axial_rope(x, row, col):
    half = x.shape[-1] // 2
    return jnp.concatenate([rope_1d(x[..., :half], row), rope_1d(x[..., half:], col)], axis=-1)


def heads(t, n_heads):
    return t.reshape(t.shape[0], t.shape[1], n_heads, A_HEAD_DIM)


def short_conv(u, w):
    pad = CONV_W // 2
    L = u.shape[1]
    up = jnp.pad(u, ((0, 0), (pad, pad), (0, 0)))
    y = up[:, 0:L] * w[0]
    for j in range(1, CONV_W):
        y = y + up[:, j:j + L] * w[j]
    return y


def window_attention(q, k, v, kc, vc, sink):
    B, L = q.shape[:2]
    n_blk = L // Q_BLOCK
    G = A_HEADS // A_KV_HEADS
    scale = A_HEAD_DIM ** -0.5
    lc = kc.shape[1]
    span = Q_BLOCK + 2 * WINDOW
    qb = q.reshape(B, n_blk, Q_BLOCK, A_KV_HEADS, G, A_HEAD_DIM).swapaxes(0, 1)
    kp = jnp.pad(k, ((0, 0), (WINDOW, WINDOW), (0, 0), (0, 0)))
    vp = jnp.pad(v, ((0, 0), (WINDOW, WINDOW), (0, 0), (0, 0)))
    rel = jnp.arange(span)[None, :] - WINDOW - jnp.arange(Q_BLOCK)[:, None]
    in_win = jnp.abs(rel) <= WINDOW
    sink_l = jnp.broadcast_to(sink.astype(jnp.float32).reshape(1, A_KV_HEADS, G, 1, 1),
                              (B, A_KV_HEADS, G, Q_BLOCK, 1))

    def one_block(args):
        n, q_n = args
        start = n * Q_BLOCK
        k_n = lax.dynamic_slice_in_dim(kp, start, span, axis=1)
        v_n = lax.dynamic_slice_in_dim(vp, start, span, axis=1)
        key_pos = start - WINDOW + jnp.arange(span)
        valid = in_win & ((key_pos >= 0) & (key_pos < L))[None, :]
        s_loc = jnp.einsum('bikgd,bjkd->bkgij', q_n, k_n).astype(jnp.float32) * scale
        s_loc = jnp.where(valid, s_loc, -jnp.inf)
        s_ctx = jnp.einsum('bikgd,bjkd->bkgij', q_n, kc).astype(jnp.float32) * scale
        p = jax.nn.softmax(jnp.concatenate([sink_l, s_ctx, s_loc], axis=-1), axis=-1).astype(v.dtype)
        return (jnp.einsum('bkgij,bjkd->bikgd', p[..., 1:1 + lc], vc)
                + jnp.einsum('bkgij,bjkd->bikgd', p[..., 1 + lc:], v_n))

    out = lax.map(one_block, (jnp.arange(n_blk), qb))
    return out.swapaxes(0, 1).reshape(B, L, A_WIDTH)


def context_attention(qc, kc, vc, sink):
    B, lc = qc.shape[:2]
    G = A_HEADS // A_KV_HEADS
    q = qc.reshape(B, lc, A_KV_HEADS, G, A_HEAD_DIM)
    s = jnp.einsum('bikgd,bjkd->bkgij', q, kc).astype(jnp.float32) * (A_HEAD_DIM ** -0.5)
    sink_l = jnp.broadcast_to(sink.astype(jnp.float32).reshape(1, A_KV_HEADS, G, 1, 1),
                              (B, A_KV_HEADS, G, lc, 1))
    p = jax.nn.softmax(jnp.concatenate([sink_l, s], axis=-1), axis=-1).astype(vc.dtype)
    return jnp.einsum('bkgij,bjkd->bikgd', p[..., 1:], vc).reshape(B, lc, A_WIDTH)


def ab_layer(x, xc, mod, mod_c, w_in, w_out, sink, conv_w, ln_g, ln_b, row, col, ctx_out):
    shift, scale, gate = jnp.split(mod, 3, axis=-1)
    shift_c, scale_c, gate_c = jnp.split(mod_c, 3, axis=-1)
    widths = (A_WIDTH, A_KV_WIDTH, A_KV_WIDTH, A_WIDTH, B_WIDTH, B_WIDTH, B_WIDTH, B_WIDTH)
    q, k, v, g_a, xb, b_g, c_g, g_b = split_cols(modulate(x, shift, scale) @ w_in, widths)
    hc = modulate(xc, shift_c, scale_c)
    if ctx_out:
        qc, kc, vc, g_ac, xbc, b_gc, c_gc, g_bc = split_cols(hc @ w_in, widths)
    else:
        kc, vc = split_cols(hc @ w_in[:, A_WIDTH:A_WIDTH + 2 * A_KV_WIDTH], (A_KV_WIDTH, A_KV_WIDTH))
    kc, vc = heads(kc, A_KV_HEADS), heads(vc, A_KV_HEADS)
    q = axial_rope(heads(q, A_HEADS), row, col)
    k = axial_rope(heads(k, A_KV_HEADS), row, col)
    o_a = window_attention(q, k, heads(v, A_KV_HEADS), kc, vc, sink)
    o_b = b_g * short_conv(c_g * xb, conv_w)
    y = jnp.concatenate([o_a * jax.nn.silu(g_a), o_b * jax.nn.silu(g_b)], axis=-1) @ w_out
    x_new = layer_norm(DEEPNORM_ALPHA * x + gate * y, ln_g, ln_b)
    if not ctx_out:
        return x_new, None
    o_ac = context_attention(heads(qc, A_HEADS), kc, vc, sink)
    o_bc = b_gc * short_conv(c_gc * xbc, conv_w)
    yc = jnp.concatenate([o_ac * jax.nn.silu(g_ac), o_bc * jax.nn.silu(g_bc)], axis=-1) @ w_out
    xc_new = layer_norm(DEEPNORM_ALPHA * xc + gate_c * yc, ln_g, ln_b)
    return x_new, xc_new


def to_heads(t, d):
    B, L, _ = t.shape
    return t.reshape(B, L, -1, d).transpose(0, 2, 1, 3).astype(jnp.float32)


def forget_terms(z, lb):
    lb = jnp.clip(lb.astype(jnp.float32), 0.0, 1.0).reshape(C_HEADS, 1, C_KEY_DIM)
    log_f = jnp.logaddexp(jnp.log(lb), jnp.log1p(-lb) + jax.nn.log_sigmoid(z))
    k = (1 - lb) * jax.nn.sigmoid(-z)
    return k, log_f


def hgrn2_scan(q, k, v, log_f, s0, with_outputs):
    B, H, L, dk = k.shape
    n = L // CHUNK
    chunks = lambda t: t.reshape(B, H, n, CHUNK, t.shape[-1])
    to_scan = lambda t: jnp.moveaxis(t, 2, 0)
    k, v, log_f = chunks(k), chunks(v), chunks(log_f)
    b = jnp.cumsum(log_f, axis=3)
    b_last = b[:, :, :, -1:, :]
    k_state = k * jnp.exp(b_last - b)
    decay = jnp.exp(b_last[:, :, :, 0, :])
    if with_outputs:
        q = chunks(q)
        q_inter = q * jnp.exp(b)
        tri = jnp.tril(jnp.ones((CHUNK, CHUNK), dtype=bool))[:, :, None]

        def step(s, xs):
            q_n, k_n, v_n, b_n, qi_n, ks_n, d_n = xs
            o_inter = jnp.einsum('bhcd,bhde->bhce', qi_n, s)
            diff = b_n[:, :, :, None, :] - b_n[:, :, None, :, :]
            w = jnp.exp(jnp.where(tri, diff, -jnp.inf))
            scores = jnp.einsum('bhcd,bhsd,bhcsd->bhcs', q_n, k_n, w)
            o_n = o_inter + jnp.einsum('bhcs,bhse->bhce', scores, v_n)
            return d_n[..., None] * s + jnp.einsum('bhcd,bhce->bhde', ks_n, v_n), o_n

        s_final, o = lax.scan(step, s0, (to_scan(q), to_scan(k), to_scan(v), to_scan(b),
                                         to_scan(q_inter), to_scan(k_state), to_scan(decay)))
        o = jnp.moveaxis(o, 0, 2).reshape(B, H, L, v.shape[-1])
        return o, s_final

    def step_state(s, xs):
        k_n, v_n, d_n = xs
        return d_n[..., None] * s + jnp.einsum('bhcd,bhce->bhde', k_n, v_n), None

    s_final, _ = lax.scan(step_state, s0, (to_scan(k_state), to_scan(v), to_scan(decay)))
    return None, s_final


def hgrn2_direction(q, i, z, lb, qc, ic, zc, ctx_out, reverse):
    flip = (lambda t: jnp.flip(t, axis=2)) if reverse else (lambda t: t)
    B = i.shape[0]
    kc, lfc = forget_terms(zc, lb)
    s0 = jnp.zeros((B, C_HEADS, C_KEY_DIM, C_VAL_DIM), jnp.float32)
    oc, s_ctx = hgrn2_scan(flip(qc) if ctx_out else None, flip(kc), flip(ic), flip(lfc), s0, ctx_out)
    k, lf = forget_terms(z, lb)
    o, _ = hgrn2_scan(flip(q), flip(k), flip(i), flip(lf), s_ctx, True)
    return flip(o), (flip(oc) if ctx_out else None)


def hgrn2_readout(o, g, g_norm, w_out):
    on = o * lax.rsqrt(jnp.mean(o * o, axis=-1, keepdims=True) + RMS_EPS) * g_norm
    B, H, L, dv = on.shape
    return (on.transpose(0, 2, 1, 3).reshape(B, L, H * dv).astype(g.dtype) * jax.nn.silu(g)) @ w_out


def c_layer(x, xc, mod, mod_c, w_in, w_out, lb_f, lb_b, g_norm, ln_g, ln_b, ctx_out):
    shift, scale, gate = jnp.split(mod, 3, axis=-1)
    shift_c, scale_c, gate_c = jnp.split(mod_c, 3, axis=-1)
    widths = (C_F_WIDTH, C_F_WIDTH, C_F_WIDTH, C_I_WIDTH, C_I_WIDTH)
    q, zf, zb, i, g = split_cols(modulate(x, shift, scale) @ w_in, widths)
    hc = modulate(xc, shift_c, scale_c)
    if ctx_out:
        qc, zfc, zbc, ic, gc = split_cols(hc @ w_in, widths)
        qc = jax.nn.silu(to_heads(qc, C_KEY_DIM)) * (C_KEY_DIM ** -0.5)
    else:
        zfc, zbc, ic = split_cols(hc @ w_in[:, C_F_WIDTH:3 * C_F_WIDTH + C_I_WIDTH],
                                  (C_F_WIDTH, C_F_WIDTH, C_I_WIDTH))
        qc = None
    q = jax.nn.silu(to_heads(q, C_KEY_DIM)) * (C_KEY_DIM ** -0.5)
    i, zf, zb = to_heads(i, C_VAL_DIM), to_heads(zf, C_KEY_DIM), to_heads(zb, C_KEY_DIM)
    ic, zfc, zbc = to_heads(ic, C_VAL_DIM), to_heads(zfc, C_KEY_DIM), to_heads(zbc, C_KEY_DIM)
    o_f, oc_f = hgrn2_direction(q, i, zf, lb_f, qc, ic, zfc, ctx_out, reverse=False)
    o_b, oc_b = hgrn2_direction(q, i, zb, lb_b, qc, ic, zbc, ctx_out, reverse=True)
    y = hgrn2_readout(o_f + o_b, g, g_norm, w_out)
    x_new = layer_norm(DEEPNORM_ALPHA * x + gate * y, ln_g, ln_b)
    if not ctx_out:
        return x_new, None
    yc = hgrn2_readout(oc_f + oc_b, gc, g_norm, w_out)
    xc_new = layer_norm(DEEPNORM_ALPHA * xc + gate_c * yc, ln_g, ln_b)
    return x_new, xc_new


def setup_inputs(seed: int = 0) -> dict:
    key = jax.random.key(seed)
    ks = jax.random.split(key, 16)
    nrm = lambda k, shape, s: jax.random.normal(k, shape, jnp.float32) * s
    ab_col_scale = jnp.concatenate([
        jnp.ones((A_WIDTH + A_KV_WIDTH,), jnp.float32),
        jnp.full((A_KV_WIDTH,), DEEPNORM_BETA, jnp.float32),
        jnp.ones((A_WIDTH + 4 * B_WIDTH,), jnp.float32)])
    c_col_scale = jnp.concatenate([
        jnp.ones((3 * C_F_WIDTH,), jnp.float32),
        jnp.full((C_I_WIDTH,), DEEPNORM_BETA, jnp.float32),
        jnp.ones((C_I_WIDTH,), jnp.float32)])
    return {
        "x": nrm(ks[0], (BATCH, SEQ, D_MODEL), 1.0),
        "c": nrm(ks[1], (BATCH, D_MODEL), 1.0),
        "ctx": nrm(ks[2], (BATCH, CTX_LEN, D_MODEL), 1.0),
        "c_ctx": nrm(ks[3], (D_MODEL,), 1.0),
        "w_ada": nrm(ks[4], (DEPTH, D_MODEL, 3 * D_MODEL), D_MODEL ** -0.5),
        "b_ada": nrm(ks[5], (DEPTH, 3 * D_MODEL), 0.02),
        "ln_g": 1.0 + nrm(ks[6], (DEPTH, D_MODEL), 0.02),
        "ln_b": nrm(ks[7], (DEPTH, D_MODEL), 0.02),
        "w_in_ab": nrm(ks[8], (N_AB, D_MODEL, AB_IN_WIDTH), D_MODEL ** -0.5) * ab_col_scale,
        "w_out_ab": nrm(ks[9], (N_AB, AB_OUT_WIDTH, D_MODEL), AB_OUT_WIDTH ** -0.5 * DEEPNORM_BETA),
        "sink_ab": nrm(ks[10], (N_AB, A_HEADS), 0.5),
        "conv_ab": nrm(ks[11], (N_AB, CONV_W, B_WIDTH), CONV_W ** -0.5),
        "w_in_c": nrm(ks[12], (N_C, D_MODEL, C_IN_WIDTH), D_MODEL ** -0.5) * c_col_scale,
        "w_out_c": nrm(ks[13], (N_C, C_I_WIDTH, D_MODEL), C_I_WIDTH ** -0.5 * DEEPNORM_BETA),
        "lb_c": nrm(ks[14], (2, N_C, C_F_WIDTH), 0.5),
        "gnorm_c": 1.0 + nrm(ks[15], (N_C, C_VAL_DIM), 0.02),
    }


def reference(x, c, ctx, c_ctx, w_ada, b_ada, ln_g, ln_b, w_in_ab, w_out_ab, sink_ab, conv_ab,
              w_in_c, w_out_c, lb_c, gnorm_c):
    L = x.shape[1]
    rows = L // GRID_W
    row = jnp.repeat(jnp.arange(rows), GRID_W)
    col = jnp.tile(jnp.arange(GRID_W), rows)
    lb_p = jax.nn.softmax(lb_c.astype(jnp.float32), axis=1)
    lb_all = jnp.cumsum(lb_p, axis=1) - lb_p[:, :1]
    silu_c = jax.nn.silu(c)
    silu_cc = jax.nn.silu(c_ctx)
    xc = ctx
    for l in range(DEPTH):
        mod = (silu_c @ w_ada[l] + b_ada[l])[:, None, :]
        mod_c = silu_cc @ w_ada[l] + b_ada[l]
        ctx_out = l < DEPTH - 1
        j = l // 2
        if l % 2 == 0:
            x, xc = ab_layer(x, xc, mod, mod_c, w_in_ab[j], w_out_ab[j], sink_ab[j], conv_ab[j],
                             ln_g[l], ln_b[l], row, col, ctx_out)
        else:
            x, xc = c_layer(x, xc, mod, mod_c, w_in_c[j], w_out_c[j], lb_all[0, j], lb_all[1, j],
                            gnorm_c[j], ln_g[l], ln_b[l], ctx_out)
    return x
```

```python
import functools

import numpy as np
import jax
import jax.numpy as jnp
from jax import lax
from jax.experimental import pallas as pl
from jax.experimental.pallas import tpu as pltpu

F32 = jnp.float32
BF16 = jnp.bfloat16

D_MODEL = 1024
DEPTH = 4
GRID_W = 64
A_HEADS = 8
A_KV_HEADS = 2
A_HEAD_DIM = 64
A_WIDTH = A_HEADS * A_HEAD_DIM
A_KV_WIDTH = A_KV_HEADS * A_HEAD_DIM
WINDOW = 128
ROPE_BASE = 10000.0
B_WIDTH = D_MODEL // 2
C_HEADS = 8
C_KEY_DIM = 128
C_VAL_DIM = D_MODEL // C_HEADS
C_WIDTH = C_HEADS * C_KEY_DIM
DEEPNORM_ALPHA = (2 * DEPTH) ** 0.25
LN_EPS = 1e-5
RMS_EPS = 1e-6

LANES = 128
SUBLANES = 8
ROW_TILE = 256
SCAN_TILE = 128
SCAN_LEVELS = (1, 2, 4, 8, 16, 32, 64)
VMEM_LIMIT = 56 * 1024 * 1024


def _params(n_axes):
    return pltpu.CompilerParams(dimension_semantics=("arbitrary",) * n_axes, vmem_limit_bytes=VMEM_LIMIT)


def _dot(a, b):
    return jnp.dot(a, b, preferred_element_type=F32)


def _dot_nt(a, b):
    return lax.dot_general(a, b, (((1,), (1,)), ((), ())), preferred_element_type=F32)


def _silu(t):
    return t * jax.nn.sigmoid(t)


def _modulate(x, mod_ref):
    shift = mod_ref[0, 0, 0:1, :]
    scale = mod_ref[0, 0, 1:2, :]
    return x * (1 + scale) + shift


def _layer_norm(r, g, b):
    mu = jnp.mean(r, axis=-1, keepdims=True)
    rc = r - mu
    var = jnp.mean(rc * rc, axis=-1, keepdims=True)
    return rc * lax.rsqrt(var + LN_EPS) * g + b


def _ada_kernel(s_ref, w_ref, b_ref, o_ref):
    s = _silu(s_ref[...]).astype(BF16)
    o_ref[0] = _dot(s, w_ref[0].astype(BF16)) + b_ref[0]


def _ada(s, w_ada, b_ada):
    depth = w_ada.shape[0]
    rows = s.shape[0]
    n_col = 3 * D_MODEL // D_MODEL
    return pl.pallas_call(
        _ada_kernel,
        grid=(depth, n_col),
        in_specs=[
            pl.BlockSpec((rows, D_MODEL), lambda l, n: (0, 0)),
            pl.BlockSpec((1, D_MODEL, D_MODEL), lambda l, n: (l, 0, n)),
            pl.BlockSpec((1, 1, D_MODEL), lambda l, n: (l, 0, n)),
        ],
        out_specs=pl.BlockSpec((1, rows, D_MODEL), lambda l, n: (l, 0, n)),
        out_shape=jax.ShapeDtypeStruct((depth, rows, 3 * D_MODEL), F32),
        compiler_params=_params(2),
        name="ada_mod",
    )(s, w_ada, b_ada.reshape(depth, 1, 3 * D_MODEL))


def _rope(t, cos, sin_signed, lane_lo):
    partner = jnp.where(lane_lo, pltpu.roll(t, LANES - 16, axis=1), pltpu.roll(t, 16, axis=1))
    return t * cos + partner * sin_signed


def _ab_in_kernel(x_ref, xp_ref, xn_ref, mod_ref, w_ref, cw_ref, cos_ref, sin_ref,
                  q_ref, k_ref, v_ref, sga_ref, obg_ref, *, n_tiles, ctx_tiles):
    i = pl.program_id(1)
    tm = x_ref.shape[1]
    h = _modulate(x_ref[0], mod_ref).astype(BF16)

    def proj(lhs, lo, hi):
        return _dot(lhs, w_ref[:, lo:hi])

    o_q, o_k, o_v, o_ga = 0, A_WIDTH, A_WIDTH + A_KV_WIDTH, A_WIDTH + 2 * A_KV_WIDTH
    o_xb = o_ga + A_WIDTH
    o_bg, o_cg, o_gb = o_xb + B_WIDTH, o_xb + 2 * B_WIDTH, o_xb + 3 * B_WIDTH

    cos = cos_ref[...]
    sin_signed = sin_ref[...]
    lane = lax.broadcasted_iota(jnp.int32, (tm, LANES), 1)
    lane_lo = (lane % 32) < 16
    q = proj(h, o_q, o_k)
    for j in range(A_WIDTH // LANES):
        qj = _rope(q[:, j * LANES:(j + 1) * LANES], cos, sin_signed, lane_lo)
        q_ref[0, :, j * LANES:(j + 1) * LANES] = qj * (A_HEAD_DIM ** -0.5)
    k_ref[0] = _rope(proj(h, o_k, o_v), cos, sin_signed, lane_lo)
    v_ref[0] = proj(h, o_v, o_ga)
    sga_ref[0] = _silu(proj(h, o_ga, o_xb))

    u = proj(h, o_cg, o_gb) * proj(h, o_xb, o_bg)
    halo = jnp.concatenate([xp_ref[0], xn_ref[0]], axis=0)
    hh = _modulate(halo, mod_ref).astype(BF16)
    uh = proj(hh, o_cg, o_gb) * proj(hh, o_xb, o_bg)
    left_ok = jnp.logical_and(i != 0, i != ctx_tiles)
    right_ok = jnp.logical_and(i != ctx_tiles - 1, i != n_tiles - 1)
    u_left = jnp.where(left_ok, uh[SUBLANES - 1:SUBLANES], 0.0)
    u_right = jnp.where(right_ok, uh[SUBLANES:SUBLANES + 1], 0.0)
    row = lax.broadcasted_iota(jnp.int32, u.shape, 0)
    u_prev = jnp.where(row == 0, u_left, pltpu.roll(u, 1, axis=0))
    u_next = jnp.where(row == tm - 1, u_right, pltpu.roll(u, tm - 1, axis=0))
    conv = u_prev * cw_ref[0:1, :] + u * cw_ref[1:2, :] + u_next * cw_ref[2:3, :]
    obg_ref[0] = proj(h, o_bg, o_cg) * conv * _silu(proj(h, o_gb, o_gb + B_WIDTH))


def _ab_in(xcat, modsel, w_in, conv_w, cos_t, sin_t, ctx_len):
    B, L, _ = xcat.shape
    tm = ROW_TILE
    n_tiles = L // tm
    ctx_tiles = ctx_len // tm
    halo_blocks = tm // SUBLANES
    n_halo = L // SUBLANES
    kern = functools.partial(_ab_in_kernel, n_tiles=n_tiles, ctx_tiles=ctx_tiles)
    width = w_in.shape[1]
    tok = lambda w: pl.BlockSpec((1, tm, w), lambda b, i: (b, i, 0))
    return pl.pallas_call(
        kern,
        grid=(B, n_tiles),
        in_specs=[
            tok(D_MODEL),
            pl.BlockSpec((1, SUBLANES, D_MODEL), lambda b, i: (b, jnp.maximum(i * halo_blocks - 1, 0), 0)),
            pl.BlockSpec((1, SUBLANES, D_MODEL), lambda b, i: (b, jnp.minimum((i + 1) * halo_blocks, n_halo - 1), 0)),
            pl.BlockSpec((1, 1, 3, D_MODEL), lambda b, i: (b, (i >= ctx_tiles).astype(jnp.int32), 0, 0)),
            pl.BlockSpec((D_MODEL, width), lambda b, i: (0, 0)),
            pl.BlockSpec((3, B_WIDTH), lambda b, i: (0, 0)),
            pl.BlockSpec((tm, LANES), lambda b, i: (i, 0)),
            pl.BlockSpec((tm, LANES), lambda b, i: (i, 0)),
        ],
        out_specs=[tok(A_WIDTH), tok(A_KV_WIDTH), tok(A_KV_WIDTH), tok(A_WIDTH), tok(B_WIDTH)],
        out_shape=[
            jax.ShapeDtypeStruct((B, L, A_WIDTH), F32),
            jax.ShapeDtypeStruct((B, L, A_KV_WIDTH), F32),
            jax.ShapeDtypeStruct((B, L, A_KV_WIDTH), F32),
            jax.ShapeDtypeStruct((B, L, A_WIDTH), F32),
            jax.ShapeDtypeStruct((B, L, B_WIDTH), F32),
        ],
        compiler_params=_params(2),
        name="ab_in",
    )(xcat, xcat, xcat, modsel, w_in, conv_w, cos_t, sin_t)


def _ab_attn_kernel(q_ref, kx_ref, kp_ref, kc_ref, kn_ref, vx_ref, vp_ref, vc_ref, vn_ref,
                    sga_ref, obg_ref, x_ref, mod_ref, sink_ref, w_ref, g_ref, b_ref, o_ref,
                    *, n_blk, ctx_blk):
    qi = pl.program_id(1)
    T = q_ref.shape[1]
    lc = kx_ref.shape[1]
    half = LANES // 2

    is_lat = qi >= ctx_blk
    prev_ok = jnp.logical_and(is_lat, qi - 1 >= ctx_blk)
    next_ok = jnp.logical_and(is_lat, qi + 1 <= n_blk - 1)
    ri = lax.broadcasted_iota(jnp.int32, (T, T), 0)
    ci = lax.broadcasted_iota(jnp.int32, (T, T), 1)
    neg = jnp.float32(-jnp.inf)
    bias = jnp.concatenate([
        jnp.zeros((T, lc), F32),
        jnp.where(jnp.logical_and(prev_ok, ci >= ri), 0.0, neg),
        jnp.where(jnp.logical_and(is_lat, ci >= 0), 0.0, neg),
        jnp.where(jnp.logical_and(next_ok, ci <= ri), 0.0, neg),
    ], axis=1)

    k_all = jnp.concatenate([kx_ref[0], kp_ref[0], kc_ref[0], kn_ref[0]], axis=0)
    v_all = jnp.concatenate([vx_ref[0], vp_ref[0], vc_ref[0], vn_ref[0]], axis=0)
    k_var = (k_all.astype(BF16), pltpu.roll(k_all, half, axis=1).astype(BF16))
    v_var = (v_all.astype(BF16), pltpu.roll(v_all, half, axis=1).astype(BF16))

    lane = lax.broadcasted_iota(jnp.int32, (T, LANES), 1)
    low = lane < half
    cols = []
    for j in range(A_HEADS // 2):
        qj = q_ref[0, :, j * LANES:(j + 1) * LANES]
        outs = []
        for p in range(2):
            hd = 2 * j + p
            kvh = hd // (A_HEADS // A_KV_HEADS)
            var = 0 if kvh == p else 1
            qm = jnp.where(low if p == 0 else jnp.logical_not(low), qj, 0.0).astype(BF16)
            s = _dot_nt(qm, k_var[var]) + bias
            sk = sink_ref[hd:hd + 1, 0:1]
            m = jnp.maximum(jnp.max(s, axis=-1, keepdims=True), sk)
            e = jnp.exp(s - m)
            denom = jnp.sum(e, axis=-1, keepdims=True) + jnp.exp(sk - m)
            outs.append(_dot(e.astype(BF16), v_var[var]) / denom)
        cols.append(jnp.where(low, outs[0], outs[1]))
    o_a = jnp.concatenate(cols, axis=1)

    y_in = jnp.concatenate([o_a * sga_ref[0], obg_ref[0]], axis=1).astype(BF16)
    y = _dot(y_in, w_ref[...])
    gate = mod_ref[0, 0, 2:3, :]
    o_ref[0] = _layer_norm(DEEPNORM_ALPHA * x_ref[0] + gate * y, g_ref[...], b_ref[...])


def _ab_attn(xcat, modsel, q, k, v, sga, obg, sink_b, w_out, ln_g, ln_b, ctx_len):
    B, L, _ = xcat.shape
    T = SCAN_TILE
    n_blk = L // T
    ctx_blk = ctx_len // T
    kern = functools.partial(_ab_attn_kernel, n_blk=n_blk, ctx_blk=ctx_blk)
    tok = lambda w: pl.BlockSpec((1, T, w), lambda b, i: (b, i, 0))
    prev = pl.BlockSpec((1, T, A_KV_WIDTH), lambda b, i: (b, jnp.maximum(i - 1, 0), 0))
    nxt = pl.BlockSpec((1, T, A_KV_WIDTH), lambda b, i: (b, jnp.minimum(i + 1, n_blk - 1), 0))
    ctxs = pl.BlockSpec((1, ctx_len, A_KV_WIDTH), lambda b, i: (b, 0, 0))
    row = pl.BlockSpec((1, D_MODEL), lambda b, i: (0, 0))
    return pl.pallas_call(
        kern,
        grid=(B, n_blk),
        in_specs=[
            tok(A_WIDTH),
            ctxs, prev, tok(A_KV_WIDTH), nxt,
            ctxs, prev, tok(A_KV_WIDTH), nxt,
            tok(A_WIDTH), tok(B_WIDTH), tok(D_MODEL),
            pl.BlockSpec((1, 1, 3, D_MODEL), lambda b, i: (b, (i >= ctx_blk).astype(jnp.int32), 0, 0)),
            pl.BlockSpec((A_HEADS, LANES), lambda b, i: (0, 0)),
            pl.BlockSpec((D_MODEL, D_MODEL), lambda b, i: (0, 0)),
            row, row,
        ],
        out_specs=tok(D_MODEL),
        out_shape=jax.ShapeDtypeStruct((B, L, D_MODEL), F32),
        compiler_params=_params(2),
        name="ab_attn_out",
    )(q, k, k, k, k, v, v, v, v, sga, obg, xcat, modsel, sink_b, w_out, ln_g, ln_b)


def _tile_cumsum(x, reverse):
    n = x.shape[0]
    row = lax.broadcasted_iota(jnp.int32, x.shape, 0)
    s = 1
    while s < n:
        if reverse:
            x = x + jnp.where(row < n - s, pltpu.roll(x, n - s, axis=0), 0.0)
        else:
            x = x + jnp.where(row >= s, pltpu.roll(x, s, axis=0), 0.0)
        s *= 2
    return x


def _forget_terms(z, lb):
    e = jnp.exp(-jnp.abs(z))
    log_sig = jnp.minimum(z, 0.0) - jnp.log1p(e)
    a = jnp.log(lb)
    c = jnp.log1p(-lb) + log_sig
    log_f = jnp.maximum(a, c) + jnp.log1p(jnp.exp(-jnp.abs(a - c)))
    r = 1.0 / (1.0 + e)
    k = (1.0 - lb) * jnp.where(z >= 0, e * r, r)
    return log_f, k


def _c_in_kernel(x_ref, mod_ref, w_ref, lb_ref, q_ref, cf_ref, kf_ref, cb_ref, kb_ref, v_ref, sg_ref):
    tm = x_ref.shape[1]
    h = _modulate(x_ref[0], mod_ref).astype(BF16)

    def proj(idx):
        return _dot(h, w_ref[:, idx * C_WIDTH:(idx + 1) * C_WIDTH])

    def put(ref, val):
        for hd in range(C_HEADS):
            ref[0, hd] = val[:, hd * LANES:(hd + 1) * LANES]

    put(q_ref, _silu(proj(0)) * (C_KEY_DIM ** -0.5))
    put(v_ref, proj(3))
    sg_ref[0] = _silu(proj(4))
    for d, (c_ref, k_ref) in enumerate(((cf_ref, kf_ref), (cb_ref, kb_ref))):
        lb = jnp.clip(lb_ref[d:d + 1, :], 0.0, 1.0)
        log_f, kk = _forget_terms(proj(1 + d), lb)
        put(k_ref, kk)
        cum = jnp.concatenate([_tile_cumsum(log_f[t:t + SCAN_TILE], d == 1)
                               for t in range(0, tm, SCAN_TILE)], axis=0)
        put(c_ref, cum)


def _c_in(xcat, modsel, w_in, lb, ctx_len):
    B, L, _ = xcat.shape
    tm = ROW_TILE
    n_tiles = L // tm
    ctx_tiles = ctx_len // tm
    width = w_in.shape[1]
    hm = pl.BlockSpec((1, C_HEADS, tm, LANES), lambda b, i: (b, 0, i, 0))
    hm_shape = jax.ShapeDtypeStruct((B, C_HEADS, L, LANES), F32)
    return pl.pallas_call(
        _c_in_kernel,
        grid=(B, n_tiles),
        in_specs=[
            pl.BlockSpec((1, tm, D_MODEL), lambda b, i: (b, i, 0)),
            pl.BlockSpec((1, 1, 3, D_MODEL), lambda b, i: (b, (i >= ctx_tiles).astype(jnp.int32), 0, 0)),
            pl.BlockSpec((D_MODEL, width), lambda b, i: (0, 0)),
            pl.BlockSpec((2, C_WIDTH), lambda b, i: (0, 0)),
        ],
        out_specs=[hm, hm, hm, hm, hm, hm, pl.BlockSpec((1, tm, C_WIDTH), lambda b, i: (b, i, 0))],
        out_shape=[hm_shape] * 6 + [jax.ShapeDtypeStruct((B, L, C_WIDTH), F32)],
        compiler_params=_params(2),
        name="c_in",
    )(xcat, modsel, w_in, lb)


def _pair_reference(c, m, reverse):
    n = c.shape[0]
    if 2 * m >= SUBLANES:
        off = m if reverse else m - 1
        return jnp.concatenate(
            [jnp.broadcast_to(c[p + off:p + off + 1, :], (2 * m, c.shape[1])) for p in range(0, n, 2 * m)], axis=0)
    row = lax.broadcasted_iota(jnp.int32, c.shape, 0)
    up = lambda t, s: pltpu.roll(t, n - s, axis=0)
    down = lambda t, s: pltpu.roll(t, s, axis=0)
    if not reverse:
        if m == 1:
            return jnp.where(row % 2 == 0, c, down(c, 1))
        y = jnp.where(row % 2 == 1, c, up(c, 1))
        return jnp.where(row % 4 < 2, y, down(y, 2))
    if m == 1:
        return jnp.where(row % 2 == 1, c, up(c, 1))
    y = jnp.where(row % 2 == 0, c, down(c, 1))
    return jnp.where(row % 4 >= 2, y, up(y, 2))


def _scan_tile(q, c, k, v, state_t, mask_ref, d):
    n = c.shape[0]
    total = c[n - 1:n, :] if d == 0 else c[0:1, :]
    scores = _dot_nt(q.astype(BF16), k.astype(BF16)) * mask_ref[d, 0]
    for lvl, m in enumerate(SCAN_LEVELS):
        e = jnp.exp(-jnp.abs(c - _pair_reference(c, m, d == 1)))
        scores = scores + _dot_nt((q * e).astype(BF16), (k * e).astype(BF16)) * mask_ref[d, lvl + 1]
    q_in = (q * jnp.exp(c)).astype(BF16)
    k_out = (k * jnp.exp(total - c)).astype(BF16)
    o = _dot_nt(q_in, state_t.astype(BF16)) + _dot(scores.astype(BF16), v.astype(BF16))
    new_state = state_t * jnp.exp(total) + _dot(v.T.astype(BF16), k_out)
    return o, new_state


def _c_scan_kernel(qf_ref, cf_ref, kf_ref, vf_ref, qb_ref, cb_ref, kb_ref, vb_ref, mask_ref,
                   of_ref, ob_ref, st_ref):
    @pl.when(pl.program_id(1) == 0)
    def _():
        st_ref[...] = jnp.zeros(st_ref.shape, F32)

    def head(hd, carry):
        dirs = ((qf_ref, cf_ref, kf_ref, vf_ref, of_ref), (qb_ref, cb_ref, kb_ref, vb_ref, ob_ref))
        for d, (q_r, c_r, k_r, v_r, o_r) in enumerate(dirs):
            o, st = _scan_tile(q_r[0, hd], c_r[0, hd], k_r[0, hd], v_r[0, hd], st_ref[d, hd], mask_ref, d)
            o_r[0, hd] = o
            st_ref[d, hd] = st
        return carry

    lax.fori_loop(0, C_HEADS, head, 0)


def _scan_masks():
    n = SCAN_TILE
    c = np.arange(n)[:, None]
    s = np.arange(n)[None, :]
    fwd = [np.eye(n, dtype=np.float32)]
    for m in SCAN_LEVELS:
        fwd.append(((c // (2 * m) == s // (2 * m)) & (c % (2 * m) >= m) & (s % (2 * m) < m)).astype(np.float32))
    fwd = np.stack(fwd)
    return jnp.asarray(np.stack([fwd, fwd.transpose(0, 2, 1)]))


def _c_scan(q, cf, kf, cb, kb, v, ctx_len):
    B, H, L, _ = q.shape
    T = SCAN_TILE
    n = L // T
    nc = ctx_len // T
    masks = _scan_masks()
    fwd = pl.BlockSpec((1, H, T, LANES), lambda b, j: (b, 0, j, 0))
    bwd = pl.BlockSpec((1, H, T, LANES), lambda b, j: (b, 0, jnp.where(j < nc, nc - 1 - j, n - 1 - (j - nc)), 0))
    hm_shape = jax.ShapeDtypeStruct((B, H, L, LANES), F32)
    return pl.pallas_call(
        _c_scan_kernel,
        grid=(B, n),
        in_specs=[fwd, fwd, fwd, fwd, bwd, bwd, bwd, bwd,
                  pl.BlockSpec(masks.shape, lambda b, j: (0, 0, 0, 0))],
        out_specs=[fwd, bwd],
        out_shape=[hm_shape, hm_shape],
        scratch_shapes=[pltpu.VMEM((2, H, C_VAL_DIM, C_KEY_DIM), F32)],
        compiler_params=_params(2),
        name="c_scan",
    )(q, cf, kf, v, q, cb, kb, v, masks)


def _c_out_kernel(of_ref, ob_ref, sg_ref, x_ref, mod_ref, gn_ref, w_ref, g_ref, b_ref, o_ref):
    cols = []
    for hd in range(C_HEADS):
        o = of_ref[0, hd] + ob_ref[0, hd]
        ms = jnp.mean(o * o, axis=-1, keepdims=True)
        cols.append(o * lax.rsqrt(ms + RMS_EPS) * gn_ref[...])
    y_in = (jnp.concatenate(cols, axis=1) * sg_ref[0]).astype(BF16)
    y = _dot(y_in, w_ref[...])
    gate = mod_ref[0, 0, 2:3, :]
    o_ref[0] = _layer_norm(DEEPNORM_ALPHA * x_ref[0] + gate * y, g_ref[...], b_ref[...])


def _c_out(xcat, modsel, o_f, o_b, sg, gnorm, w_out, ln_g, ln_b, ctx_len, latent_only):
    B, L, _ = xcat.shape
    tm = ROW_TILE
    ctx_tiles = ctx_len // tm
    skip = ctx_tiles if latent_only else 0
    n_tiles = L // tm - skip
    hm = pl.BlockSpec((1, C_HEADS, tm, LANES), lambda b, i: (b, 0, i + skip, 0))
    tok = pl.BlockSpec((1, tm, D_MODEL), lambda b, i: (b, i + skip, 0))
    row = pl.BlockSpec((1, D_MODEL), lambda b, i: (0, 0))
    return pl.pallas_call(
        _c_out_kernel,
        grid=(B, n_tiles),
        in_specs=[
            hm, hm, tok, tok,
            pl.BlockSpec((1, 1, 3, D_MODEL), lambda b, i: (b, (i + skip >= ctx_tiles).astype(jnp.int32), 0, 0)),
            pl.BlockSpec((1, LANES), lambda b, i: (0, 0)),
            pl.BlockSpec((D_MODEL, D_MODEL), lambda b, i: (0, 0)),
            row, row,
        ],
        out_specs=pl.BlockSpec((1, tm, D_MODEL), lambda b, i: (b, i, 0)),
        out_shape=jax.ShapeDtypeStruct((B, n_tiles * tm, D_MODEL), F32),
        compiler_params=_params(2),
        name="c_out",
    )(o_f, o_b, sg, xcat, modsel, gnorm, w_out, ln_g, ln_b)


def _rope_tables(seq, ctx_len):
    t = jnp.arange(seq)
    freqs = ROPE_BASE ** (-jnp.arange(16, dtype=F32) / 16)
    ang_r = (t // GRID_W).astype(F32)[:, None] * freqs[None, :]
    ang_c = (t % GRID_W).astype(F32)[:, None] * freqs[None, :]
    cos = jnp.concatenate([jnp.cos(ang_r)] * 2 + [jnp.cos(ang_c)] * 2, axis=-1)
    sin = jnp.concatenate([-jnp.sin(ang_r), jnp.sin(ang_r), -jnp.sin(ang_c), jnp.sin(ang_c)], axis=-1)
    cos = jnp.concatenate([jnp.ones((ctx_len, 64), F32), cos], axis=0)
    sin = jnp.concatenate([jnp.zeros((ctx_len, 64), F32), sin], axis=0)
    return jnp.tile(cos, (1, 2)), jnp.tile(sin, (1, 2))


def kernel(x, c, ctx, c_ctx, w_ada, b_ada, ln_g, ln_b, w_in_ab, w_out_ab, sink_ab, conv_ab,
           w_in_c, w_out_c, lb_c, gnorm_c):
    B, seq, _ = x.shape
    ctx_len = ctx.shape[1]
    depth = w_ada.shape[0]
    assert seq % ROW_TILE == 0 and ctx_len % ROW_TILE == 0 and B <= 8

    s = jnp.concatenate([c, c_ctx[None, :], jnp.zeros((16 - B - 1, D_MODEL), F32)], axis=0)
    mod_all = _ada(s, w_ada, b_ada)
    mod_lat = mod_all[:, :B].reshape(depth, B, 1, 3, D_MODEL)
    mod_ctx = jnp.broadcast_to(mod_all[:, B].reshape(depth, 1, 1, 3, D_MODEL), mod_lat.shape)
    modsel = jnp.concatenate([mod_ctx, mod_lat], axis=2)

    lb_p = jax.nn.softmax(lb_c.astype(F32), axis=1)
    lb_all = jnp.cumsum(lb_p, axis=1) - lb_p[:, :1]

    cos_t, sin_t = _rope_tables(seq, ctx_len)
    xcat = jnp.concatenate([ctx, x], axis=1)
    for l in range(depth):
        j = l // 2
        g_row, b_row = ln_g[l][None, :], ln_b[l][None, :]
        if l % 2 == 0:
            q, k, v, sga, obg = _ab_in(xcat, modsel[l], w_in_ab[j].astype(BF16), conv_ab[j], cos_t, sin_t, ctx_len)
            sink_b = jnp.broadcast_to(sink_ab[j].astype(F32)[:, None], (A_HEADS, LANES))
            xcat = _ab_attn(xcat, modsel[l], q, k, v, sga, obg, sink_b, w_out_ab[j].astype(BF16),
                            g_row, b_row, ctx_len)
        else:
            q, cf, kf, cb, kb, v, sg = _c_in(xcat, modsel[l], w_in_c[j].astype(BF16), lb_all[:, j], ctx_len)
            o_f, o_b = _c_scan(q, cf, kf, cb, kb, v, ctx_len)
            xcat = _c_out(xcat, modsel[l], o_f, o_b, sg, gnorm_c[j][None, :], w_out_c[j].astype(BF16),
                          g_row, b_row, ctx_len, latent_only=(l == depth - 1))
    return xcat if depth % 2 == 0 else xcat[:, ctx_len:]
```

```python
import functools

import numpy as np
import jax
import jax.numpy as jnp
from jax import lax
from jax.experimental import pallas as pl
from jax.experimental.pallas import tpu as pltpu

F32 = jnp.float32
BF16 = jnp.bfloat16

D_MODEL = 1024
DEPTH = 4
GRID_W = 64
A_HEADS = 8
A_KV_HEADS = 2
A_HEAD_DIM = 64
A_WIDTH = A_HEADS * A_HEAD_DIM
A_KV_WIDTH = A_KV_HEADS * A_HEAD_DIM
WINDOW = 128
ROPE_BASE = 10000.0
B_WIDTH = D_MODEL // 2
C_HEADS = 8
C_KEY_DIM = 128
C_VAL_DIM = D_MODEL // C_HEADS
C_WIDTH = C_HEADS * C_KEY_DIM
DEEPNORM_ALPHA = (2 * DEPTH) ** 0.25
LN_EPS = 1e-5
RMS_EPS = 1e-6
LOG2_E = 1.4426950408889634

LANES = 128
SUBLANES = 8
ROW_TILE = 256
SCAN_TILE = 128
SCAN_LEVELS = (1, 2, 4, 8, 16, 32, 64)
VMEM_LIMIT = 56 * 1024 * 1024


def _params(n_axes):
    return pltpu.CompilerParams(dimension_semantics=("arbitrary",) * n_axes, vmem_limit_bytes=VMEM_LIMIT)


def _dot(a, b):
    return jnp.dot(a, b, preferred_element_type=F32)


def _dot_nt(a, b):
    return lax.dot_general(a, b, (((1,), (1,)), ((), ())), preferred_element_type=F32)


def _silu(t):
    return t * jax.nn.sigmoid(t)


def _modulate(x, mod_ref):
    shift = mod_ref[0, 0, 0:1, :]
    scale = mod_ref[0, 0, 1:2, :]
    return x * (1 + scale) + shift


def _layer_norm(r, g, b):
    mu = jnp.mean(r, axis=-1, keepdims=True)
    rc = r - mu
    var = jnp.mean(rc * rc, axis=-1, keepdims=True)
    return rc * lax.rsqrt(var + LN_EPS) * g + b


def _ada_kernel(s_ref, w_ref, b_ref, o_ref):
    s = _silu(s_ref[...]).astype(BF16)
    o_ref[0] = _dot(s, w_ref[0].astype(BF16)) + b_ref[0]


def _ada(s, w_ada, b_ada):
    depth = w_ada.shape[0]
    rows = s.shape[0]
    n_col = 3 * D_MODEL // D_MODEL
    return pl.pallas_call(
        _ada_kernel,
        grid=(depth, n_col),
        in_specs=[
            pl.BlockSpec((rows, D_MODEL), lambda l, n: (0, 0)),
            pl.BlockSpec((1, D_MODEL, D_MODEL), lambda l, n: (l, 0, n)),
            pl.BlockSpec((1, 1, D_MODEL), lambda l, n: (l, 0, n)),
        ],
        out_specs=pl.BlockSpec((1, rows, D_MODEL), lambda l, n: (l, 0, n)),
        out_shape=jax.ShapeDtypeStruct((depth, rows, 3 * D_MODEL), F32),
        compiler_params=_params(2),
        name="ada_mod",
    )(s, w_ada, b_ada.reshape(depth, 1, 3 * D_MODEL))


def _rope(t, cos, sin_signed, lane_lo):
    partner = jnp.where(lane_lo, pltpu.roll(t, LANES - 16, axis=1), pltpu.roll(t, 16, axis=1))
    return t * cos + partner * sin_signed


def _ab_in_kernel(x_ref, xp_ref, xn_ref, mod_ref, w_ref, cw_ref, cos_ref, sin_ref,
                  q_ref, k_ref, v_ref, sga_ref, obg_ref, *, n_tiles, ctx_tiles):
    i = pl.program_id(1)
    tm = x_ref.shape[1]
    h = _modulate(x_ref[0], mod_ref).astype(BF16)

    def proj(lhs, lo, hi):
        return _dot(lhs, w_ref[:, lo:hi])

    o_q, o_k, o_v, o_ga = 0, A_WIDTH, A_WIDTH + A_KV_WIDTH, A_WIDTH + 2 * A_KV_WIDTH
    o_xb = o_ga + A_WIDTH
    o_bg, o_cg, o_gb = o_xb + B_WIDTH, o_xb + 2 * B_WIDTH, o_xb + 3 * B_WIDTH

    cos = cos_ref[...]
    sin_signed = sin_ref[...]
    lane = lax.broadcasted_iota(jnp.int32, (tm, LANES), 1)
    lane_lo = (lane % 32) < 16
    q = proj(h, o_q, o_k)
    for j in range(A_WIDTH // LANES):
        qj = _rope(q[:, j * LANES:(j + 1) * LANES], cos, sin_signed, lane_lo)
        q_ref[0, :, j * LANES:(j + 1) * LANES] = qj * (A_HEAD_DIM ** -0.5 * LOG2_E)
    k_ref[0] = _rope(proj(h, o_k, o_v), cos, sin_signed, lane_lo)
    v_ref[0] = proj(h, o_v, o_ga)
    sga_ref[0] = _silu(proj(h, o_ga, o_xb))

    u = proj(h, o_cg, o_gb) * proj(h, o_xb, o_bg)
    halo = jnp.concatenate([xp_ref[0], xn_ref[0]], axis=0)
    hh = _modulate(halo, mod_ref).astype(BF16)
    uh = proj(hh, o_cg, o_gb) * proj(hh, o_xb, o_bg)
    left_ok = jnp.logical_and(i != 0, i != ctx_tiles)
    right_ok = jnp.logical_and(i != ctx_tiles - 1, i != n_tiles - 1)
    u_left = jnp.where(left_ok, uh[SUBLANES - 1:SUBLANES], 0.0)
    u_right = jnp.where(right_ok, uh[SUBLANES:SUBLANES + 1], 0.0)
    row = lax.broadcasted_iota(jnp.int32, u.shape, 0)
    u_prev = jnp.where(row == 0, u_left, pltpu.roll(u, 1, axis=0))
    u_next = jnp.where(row == tm - 1, u_right, pltpu.roll(u, tm - 1, axis=0))
    conv = u_prev * cw_ref[0:1, :] + u * cw_ref[1:2, :] + u_next * cw_ref[2:3, :]
    obg_ref[0] = proj(h, o_bg, o_cg) * conv * _silu(proj(h, o_gb, o_gb + B_WIDTH))


def _ab_in(xcat, modsel, w_in, conv_w, cos_t, sin_t, ctx_len):
    B, L, _ = xcat.shape
    tm = ROW_TILE
    n_tiles = L // tm
    ctx_tiles = ctx_len // tm
    halo_blocks = tm // SUBLANES
    n_halo = L // SUBLANES
    kern = functools.partial(_ab_in_kernel, n_tiles=n_tiles, ctx_tiles=ctx_tiles)
    width = w_in.shape[1]
    tok = lambda w: pl.BlockSpec((1, tm, w), lambda b, i: (b, i, 0))
    return pl.pallas_call(
        kern,
        grid=(B, n_tiles),
        in_specs=[
            tok(D_MODEL),
            pl.BlockSpec((1, SUBLANES, D_MODEL), lambda b, i: (b, jnp.maximum(i * halo_blocks - 1, 0), 0)),
            pl.BlockSpec((1, SUBLANES, D_MODEL), lambda b, i: (b, jnp.minimum((i + 1) * halo_blocks, n_halo - 1), 0)),
            pl.BlockSpec((1, 1, 3, D_MODEL), lambda b, i: (b, (i >= ctx_tiles).astype(jnp.int32), 0, 0)),
            pl.BlockSpec((D_MODEL, width), lambda b, i: (0, 0)),
            pl.BlockSpec((3, B_WIDTH), lambda b, i: (0, 0)),
            pl.BlockSpec((tm, LANES), lambda b, i: (i, 0)),
            pl.BlockSpec((tm, LANES), lambda b, i: (i, 0)),
        ],
        out_specs=[tok(A_WIDTH), tok(A_KV_WIDTH), tok(A_KV_WIDTH), tok(A_WIDTH), tok(B_WIDTH)],
        out_shape=[
            jax.ShapeDtypeStruct((B, L, A_WIDTH), F32),
            jax.ShapeDtypeStruct((B, L, A_KV_WIDTH), F32),
            jax.ShapeDtypeStruct((B, L, A_KV_WIDTH), F32),
            jax.ShapeDtypeStruct((B, L, A_WIDTH), F32),
            jax.ShapeDtypeStruct((B, L, B_WIDTH), F32),
        ],
        compiler_params=_params(2),
        name="ab_in",
    )(xcat, xcat, xcat, modsel, w_in, conv_w, cos_t, sin_t)


def _ab_attn_kernel(q_ref, kx_ref, kp_ref, kc_ref, kn_ref, vx_ref, vp_ref, vc_ref, vn_ref,
                    sga_ref, obg_ref, x_ref, mod_ref, sink_ref, w_ref, g_ref, b_ref, o_ref,
                    *, n_blk, ctx_blk):
    i = pl.program_id(1)
    W = WINDOW
    n_sub = q_ref.shape[1] // W
    lc = kx_ref.shape[1]
    half = LANES // 2
    group = A_HEADS // A_KV_HEADS

    k_ctx, v_ctx = kx_ref[0].astype(BF16), vx_ref[0].astype(BF16)
    k_loc = jnp.concatenate([kp_ref[0], kc_ref[0], kn_ref[0]], axis=0).astype(BF16)
    v_loc = jnp.concatenate([vp_ref[0], vc_ref[0], vn_ref[0]], axis=0).astype(BF16)

    is_lat = i * n_sub >= ctx_blk
    ri = lax.broadcasted_iota(jnp.int32, (W, W), 0)
    ci = lax.broadcasted_iota(jnp.int32, (W, W), 1)
    neg = jnp.float32(-jnp.inf)
    lane = lax.broadcasted_iota(jnp.int32, (W, LANES), 1)
    low = lane < half

    rows_out = []
    for sb in range(n_sub):
        blk = i * n_sub + sb
        prev_ok = jnp.logical_and(is_lat, blk - 1 >= ctx_blk)
        next_ok = jnp.logical_and(is_lat, blk + 1 <= n_blk - 1)
        bias = jnp.concatenate([
            jnp.where(jnp.logical_and(prev_ok, ci >= ri), 0.0, neg),
            jnp.where(jnp.logical_and(is_lat, ci >= 0), 0.0, neg),
            jnp.where(jnp.logical_and(next_ok, ci <= ri), 0.0, neg),
        ], axis=1)
        bias = jnp.concatenate([bias] * group, axis=0)
        k_sb = jnp.concatenate([k_ctx, k_loc[sb * W:(sb + 3) * W]], axis=0)
        v_sb = jnp.concatenate([v_ctx, v_loc[sb * W:(sb + 3) * W]], axis=0)
        pieces = [None] * A_HEADS
        for kvh in range(A_KV_HEADS):
            on_half = low if kvh == 0 else jnp.logical_not(low)
            stack, sinks = [], []
            for g in range(group):
                hd = kvh * group + g
                qh = q_ref[0, sb * W:(sb + 1) * W, (hd // 2) * LANES:(hd // 2 + 1) * LANES]
                if hd % 2 != kvh:
                    qh = pltpu.roll(qh, half, axis=1)
                stack.append(jnp.where(on_half, qh, 0.0))
                sinks.append(jnp.broadcast_to(sink_ref[hd:hd + 1, :] * LOG2_E, (W, LANES)))
            qs = jnp.concatenate(stack, axis=0).astype(BF16)
            sk = jnp.concatenate(sinks, axis=0)
            s = _dot_nt(qs, k_sb)
            s = jnp.concatenate([s[:, :lc], s[:, lc:] + bias], axis=1)
            m = jnp.maximum(jnp.max(s, axis=-1, keepdims=True), sk)
            e = jnp.exp2(s - jnp.concatenate([m] * (s.shape[1] // LANES), axis=1))
            denom = jnp.sum(e, axis=-1, keepdims=True) + jnp.exp2(sk - m)
            pv = _dot(e.astype(BF16), v_sb) / denom
            for g in range(group):
                hd = kvh * group + g
                piece = pv[g * W:(g + 1) * W]
                pieces[hd] = piece if hd % 2 == kvh else pltpu.roll(piece, half, axis=1)
        rows_out.append(jnp.concatenate(
            [jnp.where(low, pieces[2 * j], pieces[2 * j + 1]) for j in range(A_HEADS // 2)], axis=1))
    o_a = jnp.concatenate(rows_out, axis=0)

    y_in = jnp.concatenate([o_a * sga_ref[0], obg_ref[0]], axis=1).astype(BF16)
    y = _dot(y_in, w_ref[...])
    gate = mod_ref[0, 0, 2:3, :]
    o_ref[0] = _layer_norm(DEEPNORM_ALPHA * x_ref[0] + gate * y, g_ref[...], b_ref[...])


def _ab_attn(xcat, modsel, q, k, v, sga, obg, sink_b, w_out, ln_g, ln_b, ctx_len):
    B, L, _ = xcat.shape
    T = ROW_TILE
    W = WINDOW
    n_sub = T // W
    n_blk = L // W
    ctx_blk = ctx_len // W
    kern = functools.partial(_ab_attn_kernel, n_blk=n_blk, ctx_blk=ctx_blk)
    tok = lambda w: pl.BlockSpec((1, T, w), lambda b, i: (b, i, 0))
    prev = pl.BlockSpec((1, W, A_KV_WIDTH), lambda b, i: (b, jnp.maximum(i * n_sub - 1, 0), 0))
    nxt = pl.BlockSpec((1, W, A_KV_WIDTH), lambda b, i: (b, jnp.minimum((i + 1) * n_sub, n_blk - 1), 0))
    ctxs = pl.BlockSpec((1, ctx_len, A_KV_WIDTH), lambda b, i: (b, 0, 0))
    row = pl.BlockSpec((1, D_MODEL), lambda b, i: (0, 0))
    return pl.pallas_call(
        kern,
        grid=(B, L // T),
        in_specs=[
            tok(A_WIDTH),
            ctxs, prev, tok(A_KV_WIDTH), nxt,
            ctxs, prev, tok(A_KV_WIDTH), nxt,
            tok(A_WIDTH), tok(B_WIDTH), tok(D_MODEL),
            pl.BlockSpec((1, 1, 3, D_MODEL), lambda b, i: (b, (i * n_sub >= ctx_blk).astype(jnp.int32), 0, 0)),
            pl.BlockSpec((A_HEADS, LANES), lambda b, i: (0, 0)),
            pl.BlockSpec((D_MODEL, D_MODEL), lambda b, i: (0, 0)),
            row, row,
        ],
        out_specs=tok(D_MODEL),
        out_shape=jax.ShapeDtypeStruct((B, L, D_MODEL), F32),
        compiler_params=_params(2),
        name="ab_attn_out",
    )(q, k, k, k, k, v, v, v, v, sga, obg, xcat, modsel, sink_b, w_out, ln_g, ln_b)


def _tile_cumsum(x, reverse):
    n, width = x.shape
    assert SUBLANES == 8
    g = x.reshape(n // SUBLANES, SUBLANES, width)
    sub = lax.broadcasted_iota(jnp.int32, g.shape, 1)
    row = lambda r: g[:, r:r + 1, :]
    if reverse:
        g = g + jnp.where(sub % 2 == 0, pltpu.roll(g, SUBLANES - 1, axis=1), 0.0)
        g = g + jnp.where(sub < 2, row(2), jnp.where(jnp.logical_and(sub >= 4, sub < 6), row(6), 0.0))
        g = g + jnp.where(sub < 4, row(4), 0.0)
    else:
        g = g + jnp.where(sub % 2 == 1, pltpu.roll(g, 1, axis=1), 0.0)
        g = g + jnp.where(jnp.logical_and(sub >= 2, sub < 4), row(1), jnp.where(sub >= 6, row(5), 0.0))
        g = g + jnp.where(sub >= 4, row(3), 0.0)
    x = g.reshape(n, width)
    m = SUBLANES
    while m < n:
        parts = []
        for p in range(0, n, 2 * m):
            near, far = x[p:p + m], x[p + m:p + 2 * m]
            if reverse:
                parts += [near + far[0:1], far]
            else:
                parts += [near, far + near[m - 1:m]]
        x = jnp.concatenate(parts, axis=0)
        m *= 2
    return x


def _forget_terms(z, lb):
    e = jnp.exp(-jnp.abs(z))
    pos = z >= 0
    r = 1.0 / (1.0 + e)
    num = jnp.where(pos, 1.0 + lb * e, lb + e)
    log2_f = jnp.where(num == 0.0, z * LOG2_E, jnp.log2(num * r))
    k = (1.0 - lb) * jnp.where(pos, e * r, r)
    return log2_f, k


def _c_in_kernel(x_ref, mod_ref, w_ref, lb_ref, q_ref, cf_ref, kf_ref, cb_ref, kb_ref, v_ref, sg_ref):
    tm = x_ref.shape[1]
    h = _modulate(x_ref[0], mod_ref).astype(BF16)

    def proj(idx):
        return _dot(h, w_ref[:, idx * C_WIDTH:(idx + 1) * C_WIDTH])

    def put(ref, val):
        for hd in range(C_HEADS):
            ref[0, hd] = val[:, hd * LANES:(hd + 1) * LANES]

    put(q_ref, _silu(proj(0)) * (C_KEY_DIM ** -0.5))
    put(v_ref, proj(3))
    sg_ref[0] = _silu(proj(4))
    for d, (c_ref, k_ref) in enumerate(((cf_ref, kf_ref), (cb_ref, kb_ref))):
        lb = jnp.clip(lb_ref[d:d + 1, :], 0.0, 1.0)
        log_f, kk = _forget_terms(proj(1 + d), lb)
        put(k_ref, kk)
        cum = jnp.concatenate([_tile_cumsum(log_f[t:t + SCAN_TILE], d == 1)
                               for t in range(0, tm, SCAN_TILE)], axis=0)
        put(c_ref, cum)


def _c_in(xcat, modsel, w_in, lb, ctx_len):
    B, L, _ = xcat.shape
    tm = ROW_TILE
    n_tiles = L // tm
    ctx_tiles = ctx_len // tm
    width = w_in.shape[1]
    hm = pl.BlockSpec((1, C_HEADS, tm, LANES), lambda b, i: (b, 0, i, 0))
    hm_shape = jax.ShapeDtypeStruct((B, C_HEADS, L, LANES), F32)
    return pl.pallas_call(
        _c_in_kernel,
        grid=(B, n_tiles),
        in_specs=[
            pl.BlockSpec((1, tm, D_MODEL), lambda b, i: (b, i, 0)),
            pl.BlockSpec((1, 1, 3, D_MODEL), lambda b, i: (b, (i >= ctx_tiles).astype(jnp.int32), 0, 0)),
            pl.BlockSpec((D_MODEL, width), lambda b, i: (0, 0)),
            pl.BlockSpec((2, C_WIDTH), lambda b, i: (0, 0)),
        ],
        out_specs=[hm, hm, hm, hm, hm, hm, pl.BlockSpec((1, tm, C_WIDTH), lambda b, i: (b, i, 0))],
        out_shape=[hm_shape] * 6 + [jax.ShapeDtypeStruct((B, L, C_WIDTH), F32)],
        compiler_params=_params(2),
        name="c_in",
    )(xcat, modsel, w_in, lb)


def _pair_delta(c, m, reverse):
    n = c.shape[0]
    if 2 * m >= SUBLANES:
        off = m if reverse else m - 1
        return jnp.concatenate([c[p:p + 2 * m] - c[p + off:p + off + 1] for p in range(0, n, 2 * m)], axis=0)
    row = lax.broadcasted_iota(jnp.int32, c.shape, 0)
    up = lambda t, s: pltpu.roll(t, n - s, axis=0)
    down = lambda t, s: pltpu.roll(t, s, axis=0)
    if m == 1:
        ref = jnp.where(row % 2 == 1, c, up(c, 1)) if reverse else jnp.where(row % 2 == 0, c, down(c, 1))
    elif reverse:
        y = jnp.where(row % 2 == 0, c, down(c, 1))
        ref = jnp.where(row % 4 >= 2, y, up(y, 2))
    else:
        y = jnp.where(row % 2 == 1, c, up(c, 1))
        ref = jnp.where(row % 4 < 2, y, down(y, 2))
    return c - ref


def _pair_rows(q, k, m, reverse):
    n = q.shape[0]
    first, second = (q, k) if reverse else (k, q)
    if m >= SUBLANES:
        return jnp.concatenate([(first if (p // m) % 2 == 0 else second)[p:p + m] for p in range(0, n, m)], axis=0)
    row = lax.broadcasted_iota(jnp.int32, q.shape, 0)
    return jnp.where(row % (2 * m) < m, first, second)


def _neg_abs(t):
    sign = jnp.int32(-2 ** 31)
    return lax.bitcast_convert_type(lax.bitcast_convert_type(t, jnp.int32) | sign, F32)


def _scan_tile(q, c, k, v, state_t, mask_ref, d):
    n = c.shape[0]
    total = c[n - 1:n, :] if d == 0 else c[0:1, :]
    scores = None
    for lvl, m in enumerate(SCAN_LEVELS):
        z = (_pair_rows(q, k, m, d == 1) * jnp.exp2(_neg_abs(_pair_delta(c, m, d == 1)))).astype(BF16)
        term = _dot_nt(z, z) * mask_ref[d, lvl]
        scores = term if scores is None else scores + term
    q_in = (q * jnp.exp2(c)).astype(BF16)
    k_out = (k * jnp.exp2(total - c)).astype(BF16)
    own = jnp.sum(q * k, axis=-1, keepdims=True)
    o = _dot_nt(q_in, state_t.astype(BF16)) + _dot(scores.astype(BF16), v.astype(BF16)) + own * v
    new_state = state_t * jnp.exp2(total) + _dot(v.T.astype(BF16), k_out)
    return o, new_state


def _c_scan_kernel(qf_ref, cf_ref, kf_ref, vf_ref, qb_ref, cb_ref, kb_ref, vb_ref, mask_ref,
                   of_ref, ob_ref, st_ref):
    @pl.when(pl.program_id(1) == 0)
    def _():
        st_ref[...] = jnp.zeros(st_ref.shape, F32)

    dirs = ((qf_ref, cf_ref, kf_ref, vf_ref, of_ref), (qb_ref, cb_ref, kb_ref, vb_ref, ob_ref))
    for hd in range(C_HEADS):
        for d, (q_r, c_r, k_r, v_r, o_r) in enumerate(dirs):
            o, st = _scan_tile(q_r[0, hd], c_r[0, hd], k_r[0, hd], v_r[0, hd], st_ref[d, hd], mask_ref, d)
            o_r[0, hd] = o
            st_ref[d, hd] = st


def _scan_masks():
    n = SCAN_TILE
    c = np.arange(n)[:, None]
    s = np.arange(n)[None, :]
    fwd = []
    for m in SCAN_LEVELS:
        fwd.append(((c // (2 * m) == s // (2 * m)) & (c % (2 * m) >= m) & (s % (2 * m) < m)).astype(np.float32))
    fwd = np.stack(fwd)
    return jnp.asarray(np.stack([fwd, fwd.transpose(0, 2, 1)]))


def _c_scan(q, cf, kf, cb, kb, v, ctx_len):
    B, H, L, _ = q.shape
    T = SCAN_TILE
    n = L // T
    nc = ctx_len // T
    masks = _scan_masks()
    fwd = pl.BlockSpec((1, H, T, LANES), lambda b, j: (b, 0, j, 0))
    bwd = pl.BlockSpec((1, H, T, LANES), lambda b, j: (b, 0, jnp.where(j < nc, nc - 1 - j, n - 1 - (j - nc)), 0))
    hm_shape = jax.ShapeDtypeStruct((B, H, L, LANES), F32)
    return pl.pallas_call(
        _c_scan_kernel,
        grid=(B, n),
        in_specs=[fwd, fwd, fwd, fwd, bwd, bwd, bwd, bwd,
                  pl.BlockSpec(masks.shape, lambda b, j: (0, 0, 0, 0))],
        out_specs=[fwd, bwd],
        out_shape=[hm_shape, hm_shape],
        scratch_shapes=[pltpu.VMEM((2, H, C_VAL_DIM, C_KEY_DIM), F32)],
        compiler_params=_params(2),
        name="c_scan",
    )(q, cf, kf, v, q, cb, kb, v, masks)


def _c_out_kernel(of_ref, ob_ref, sg_ref, x_ref, mod_ref, gn_ref, w_ref, g_ref, b_ref, o_ref):
    cols = []
    for hd in range(C_HEADS):
        o = of_ref[0, hd] + ob_ref[0, hd]
        ms = jnp.mean(o * o, axis=-1, keepdims=True)
        cols.append(o * lax.rsqrt(ms + RMS_EPS) * gn_ref[...])
    y_in = (jnp.concatenate(cols, axis=1) * sg_ref[0]).astype(BF16)
    y = _dot(y_in, w_ref[...])
    gate = mod_ref[0, 0, 2:3, :]
    o_ref[0] = _layer_norm(DEEPNORM_ALPHA * x_ref[0] + gate * y, g_ref[...], b_ref[...])


def _c_out(xcat, modsel, o_f, o_b, sg, gnorm, w_out, ln_g, ln_b, ctx_len, latent_only):
    B, L, _ = xcat.shape
    tm = ROW_TILE
    ctx_tiles = ctx_len // tm
    skip = ctx_tiles if latent_only else 0
    n_tiles = L // tm - skip
    hm = pl.BlockSpec((1, C_HEADS, tm, LANES), lambda b, i: (b, 0, i + skip, 0))
    tok = pl.BlockSpec((1, tm, D_MODEL), lambda b, i: (b, i + skip, 0))
    row = pl.BlockSpec((1, D_MODEL), lambda b, i: (0, 0))
    return pl.pallas_call(
        _c_out_kernel,
        grid=(B, n_tiles),
        in_specs=[
            hm, hm, tok, tok,
            pl.BlockSpec((1, 1, 3, D_MODEL), lambda b, i: (b, (i + skip >= ctx_tiles).astype(jnp.int32), 0, 0)),
            pl.BlockSpec((1, LANES), lambda b, i: (0, 0)),
            pl.BlockSpec((D_MODEL, D_MODEL), lambda b, i: (0, 0)),
            row, row,
        ],
        out_specs=pl.BlockSpec((1, tm, D_MODEL), lambda b, i: (b, i, 0)),
        out_shape=jax.ShapeDtypeStruct((B, n_tiles * tm, D_MODEL), F32),
        compiler_params=_params(2),
        name="c_out",
    )(o_f, o_b, sg, xcat, modsel, gnorm, w_out, ln_g, ln_b)


def _rope_tables(seq, ctx_len):
    t = jnp.arange(seq)
    freqs = ROPE_BASE ** (-jnp.arange(16, dtype=F32) / 16)
    ang_r = (t // GRID_W).astype(F32)[:, None] * freqs[None, :]
    ang_c = (t % GRID_W).astype(F32)[:, None] * freqs[None, :]
    cos = jnp.concatenate([jnp.cos(ang_r)] * 2 + [jnp.cos(ang_c)] * 2, axis=-1)
    sin = jnp.concatenate([-jnp.sin(ang_r), jnp.sin(ang_r), -jnp.sin(ang_c), jnp.sin(ang_c)], axis=-1)
    cos = jnp.concatenate([jnp.ones((ctx_len, 64), F32), cos], axis=0)
    sin = jnp.concatenate([jnp.zeros((ctx_len, 64), F32), sin], axis=0)
    return jnp.tile(cos, (1, 2)), jnp.tile(sin, (1, 2))


def kernel(x, c, ctx, c_ctx, w_ada, b_ada, ln_g, ln_b, w_in_ab, w_out_ab, sink_ab, conv_ab,
           w_in_c, w_out_c, lb_c, gnorm_c):
    B, seq, _ = x.shape
    ctx_len = ctx.shape[1]
    depth = w_ada.shape[0]
    assert seq % ROW_TILE == 0 and ctx_len % ROW_TILE == 0 and B <= 8

    s = jnp.concatenate([c, c_ctx[None, :], jnp.zeros((16 - B - 1, D_MODEL), F32)], axis=0)
    mod_all = _ada(s, w_ada, b_ada)
    mod_lat = mod_all[:, :B].reshape(depth, B, 1, 3, D_MODEL)
    mod_ctx = jnp.broadcast_to(mod_all[:, B].reshape(depth, 1, 1, 3, D_MODEL), mod_lat.shape)
    modsel = jnp.concatenate([mod_ctx, mod_lat], axis=2)

    lb_p = jax.nn.softmax(lb_c.astype(F32), axis=1)
    lb_all = jnp.cumsum(lb_p, axis=1) - lb_p[:, :1]

    cos_t, sin_t = _rope_tables(seq, ctx_len)
    xcat = jnp.concatenate([ctx, x], axis=1)
    for l in range(depth):
        j = l // 2
        g_row, b_row = ln_g[l][None, :], ln_b[l][None, :]
        if l % 2 == 0:
            q, k, v, sga, obg = _ab_in(xcat, modsel[l], w_in_ab[j].astype(BF16), conv_ab[j], cos_t, sin_t, ctx_len)
            sink_b = jnp.broadcast_to(sink_ab[j].astype(F32)[:, None], (A_HEADS, LANES))
            xcat = _ab_attn(xcat, modsel[l], q, k, v, sga, obg, sink_b, w_out_ab[j].astype(BF16),
                            g_row, b_row, ctx_len)
        else:
            q, cf, kf, cb, kb, v, sg = _c_in(xcat, modsel[l], w_in_c[j].astype(BF16), lb_all[:, j], ctx_len)
            o_f, o_b = _c_scan(q, cf, kf, cb, kb, v, ctx_len)
            xcat = _c_out(xcat, modsel[l], o_f, o_b, sg, gnorm_c[j][None, :], w_out_c[j].astype(BF16),
                          g_row, b_row, ctx_len, latent_only=(l == depth - 1))
    return xcat if depth % 2 == 0 else xcat[:, ctx_len:]
```

```python
import functools

import numpy as np
import jax
import jax.numpy as jnp
from jax import lax
from jax.experimental import pallas as pl
from jax.experimental.pallas import tpu as pltpu

F32 = jnp.float32
BF16 = jnp.bfloat16

D_MODEL = 1024
DEPTH = 4
GRID_W = 64
A_HEADS = 8
A_KV_HEADS = 2
A_HEAD_DIM = 64
A_WIDTH = A_HEADS * A_HEAD_DIM
A_KV_WIDTH = A_KV_HEADS * A_HEAD_DIM
WINDOW = 128
ROPE_BASE = 10000.0
B_WIDTH = D_MODEL // 2
C_HEADS = 8
C_KEY_DIM = 128
C_VAL_DIM = D_MODEL // C_HEADS
C_WIDTH = C_HEADS * C_KEY_DIM
DEEPNORM_ALPHA = (2 * DEPTH) ** 0.25
LN_EPS = 1e-5
RMS_EPS = 1e-6
LOG2_E = 1.4426950408889634

LANES = 128
SUBLANES = 8
ROW_TILE = 256
SCAN_TILE = 128
SCAN_LEVELS = (1, 2, 4, 8, 16, 32, 64)
VMEM_LIMIT = 56 * 1024 * 1024


def _params(n_axes):
    return pltpu.CompilerParams(dimension_semantics=("arbitrary",) * n_axes, vmem_limit_bytes=VMEM_LIMIT)


def _dot(a, b):
    return jnp.dot(a, b, preferred_element_type=F32)


def _dot_nt(a, b):
    return lax.dot_general(a, b, (((1,), (1,)), ((), ())), preferred_element_type=F32)


def _silu(t):
    return t * jax.nn.sigmoid(t)


def _modulate(x, mod_ref):
    shift = mod_ref[0, 0, 0:1, :]
    scale = mod_ref[0, 0, 1:2, :]
    return x * (1 + scale) + shift


def _layer_norm(r, g, b):
    mu = jnp.mean(r, axis=-1, keepdims=True)
    rc = r - mu
    var = jnp.mean(rc * rc, axis=-1, keepdims=True)
    return rc * lax.rsqrt(var + LN_EPS) * g + b


def _ada_kernel(s_ref, w_ref, b_ref, o_ref):
    s = _silu(s_ref[...]).astype(BF16)
    o_ref[0] = _dot(s, w_ref[0].astype(BF16)) + b_ref[0]


def _ada(s, w_ada, b_ada):
    depth = w_ada.shape[0]
    rows = s.shape[0]
    n_col = 3 * D_MODEL // D_MODEL
    return pl.pallas_call(
        _ada_kernel,
        grid=(depth, n_col),
        in_specs=[
            pl.BlockSpec((rows, D_MODEL), lambda l, n: (0, 0)),
            pl.BlockSpec((1, D_MODEL, D_MODEL), lambda l, n: (l, 0, n)),
            pl.BlockSpec((1, 1, D_MODEL), lambda l, n: (l, 0, n)),
        ],
        out_specs=pl.BlockSpec((1, rows, D_MODEL), lambda l, n: (l, 0, n)),
        out_shape=jax.ShapeDtypeStruct((depth, rows, 3 * D_MODEL), F32),
        compiler_params=_params(2),
        name="ada_mod",
    )(s, w_ada, b_ada.reshape(depth, 1, 3 * D_MODEL))


def _rope(t, cos, sin_signed, lane_lo):
    partner = jnp.where(lane_lo, pltpu.roll(t, LANES - 16, axis=1), pltpu.roll(t, 16, axis=1))
    return t * cos + partner * sin_signed


def _ab_in_kernel(x_ref, xp_ref, xn_ref, mod_ref, w_ref, cw_ref, cos_ref, sin_ref,
                  q_ref, k_ref, v_ref, sga_ref, obg_ref, *, n_tiles, ctx_tiles):
    i = pl.program_id(1)
    tm = x_ref.shape[1]
    h = _modulate(x_ref[0], mod_ref).astype(BF16)

    def proj(lhs, lo, hi):
        return _dot(lhs, w_ref[:, lo:hi])

    o_q, o_k, o_v, o_ga = 0, A_WIDTH, A_WIDTH + A_KV_WIDTH, A_WIDTH + 2 * A_KV_WIDTH
    o_xb = o_ga + A_WIDTH
    o_bg, o_cg, o_gb = o_xb + B_WIDTH, o_xb + 2 * B_WIDTH, o_xb + 3 * B_WIDTH

    cos = cos_ref[...]
    sin_signed = sin_ref[...]
    lane = lax.broadcasted_iota(jnp.int32, (tm, LANES), 1)
    lane_lo = (lane % 32) < 16
    q = proj(h, o_q, o_k)
    for j in range(A_WIDTH // LANES):
        qj = _rope(q[:, j * LANES:(j + 1) * LANES], cos, sin_signed, lane_lo)
        q_ref[0, :, j * LANES:(j + 1) * LANES] = qj * (A_HEAD_DIM ** -0.5 * LOG2_E)
    k_ref[0] = _rope(proj(h, o_k, o_v), cos, sin_signed, lane_lo)
    v_ref[0] = proj(h, o_v, o_ga)
    sga_ref[0] = _silu(proj(h, o_ga, o_xb))

    u = proj(h, o_cg, o_gb) * proj(h, o_xb, o_bg)
    halo = jnp.concatenate([xp_ref[0], xn_ref[0]], axis=0)
    hh = _modulate(halo, mod_ref).astype(BF16)
    uh = proj(hh, o_cg, o_gb) * proj(hh, o_xb, o_bg)
    left_ok = jnp.logical_and(i != 0, i != ctx_tiles)
    right_ok = jnp.logical_and(i != ctx_tiles - 1, i != n_tiles - 1)
    u_left = jnp.where(left_ok, uh[SUBLANES - 1:SUBLANES], 0.0)
    u_right = jnp.where(right_ok, uh[SUBLANES:SUBLANES + 1], 0.0)
    row = lax.broadcasted_iota(jnp.int32, u.shape, 0)
    u_prev = jnp.where(row == 0, u_left, pltpu.roll(u, 1, axis=0))
    u_next = jnp.where(row == tm - 1, u_right, pltpu.roll(u, tm - 1, axis=0))
    conv = u_prev * cw_ref[0:1, :] + u * cw_ref[1:2, :] + u_next * cw_ref[2:3, :]
    obg_ref[0] = proj(h, o_bg, o_cg) * conv * _silu(proj(h, o_gb, o_gb + B_WIDTH))


def _ab_in(xcat, modsel, w_in, conv_w, cos_t, sin_t, ctx_len):
    B, L, _ = xcat.shape
    tm = ROW_TILE
    n_tiles = L // tm
    ctx_tiles = ctx_len // tm
    halo_blocks = tm // SUBLANES
    n_halo = L // SUBLANES
    kern = functools.partial(_ab_in_kernel, n_tiles=n_tiles, ctx_tiles=ctx_tiles)
    width = w_in.shape[1]
    tok = lambda w: pl.BlockSpec((1, tm, w), lambda b, i: (b, i, 0))
    return pl.pallas_call(
        kern,
        grid=(B, n_tiles),
        in_specs=[
            tok(D_MODEL),
            pl.BlockSpec((1, SUBLANES, D_MODEL), lambda b, i: (b, jnp.maximum(i * halo_blocks - 1, 0), 0)),
            pl.BlockSpec((1, SUBLANES, D_MODEL), lambda b, i: (b, jnp.minimum((i + 1) * halo_blocks, n_halo - 1), 0)),
            pl.BlockSpec((1, 1, 3, D_MODEL), lambda b, i: (b, (i >= ctx_tiles).astype(jnp.int32), 0, 0)),
            pl.BlockSpec((D_MODEL, width), lambda b, i: (0, 0)),
            pl.BlockSpec((3, B_WIDTH), lambda b, i: (0, 0)),
            pl.BlockSpec((tm, LANES), lambda b, i: (i, 0)),
            pl.BlockSpec((tm, LANES), lambda b, i: (i, 0)),
        ],
        out_specs=[tok(A_WIDTH), tok(A_KV_WIDTH), tok(A_KV_WIDTH), tok(A_WIDTH), tok(B_WIDTH)],
        out_shape=[
            jax.ShapeDtypeStruct((B, L, A_WIDTH), F32),
            jax.ShapeDtypeStruct((B, L, A_KV_WIDTH), F32),
            jax.ShapeDtypeStruct((B, L, A_KV_WIDTH), F32),
            jax.ShapeDtypeStruct((B, L, A_WIDTH), F32),
            jax.ShapeDtypeStruct((B, L, B_WIDTH), F32),
        ],
        compiler_params=_params(2),
        name="ab_in",
    )(xcat, xcat, xcat, modsel, w_in, conv_w, cos_t, sin_t)


def _ab_attn_kernel(q_ref, kx_ref, kp_ref, kc_ref, kn_ref, vx_ref, vp_ref, vc_ref, vn_ref,
                    sga_ref, obg_ref, x_ref, mod_ref, sink_ref, w_ref, g_ref, b_ref, o_ref,
                    *, n_blk, ctx_blk):
    i = pl.program_id(1)
    W = WINDOW
    n_sub = q_ref.shape[1] // W
    lc = kx_ref.shape[1]
    half = LANES // 2
    group = A_HEADS // A_KV_HEADS

    k_ctx, v_ctx = kx_ref[0].astype(BF16), vx_ref[0].astype(BF16)
    k_loc = jnp.concatenate([kp_ref[0], kc_ref[0], kn_ref[0]], axis=0).astype(BF16)
    v_loc = jnp.concatenate([vp_ref[0], vc_ref[0], vn_ref[0]], axis=0).astype(BF16)

    is_lat = i * n_sub >= ctx_blk
    ri = lax.broadcasted_iota(jnp.int32, (W, W), 0)
    ci = lax.broadcasted_iota(jnp.int32, (W, W), 1)
    neg = jnp.float32(-jnp.inf)
    lane = lax.broadcasted_iota(jnp.int32, (W, LANES), 1)
    low = lane < half

    rows_out = []
    for sb in range(n_sub):
        blk = i * n_sub + sb
        prev_ok = jnp.logical_and(is_lat, blk - 1 >= ctx_blk)
        next_ok = jnp.logical_and(is_lat, blk + 1 <= n_blk - 1)
        bias = jnp.concatenate([
            jnp.where(jnp.logical_and(prev_ok, ci >= ri), 0.0, neg),
            jnp.where(jnp.logical_and(is_lat, ci >= 0), 0.0, neg),
            jnp.where(jnp.logical_and(next_ok, ci <= ri), 0.0, neg),
        ], axis=1)
        bias = jnp.concatenate([bias] * group, axis=0)
        k_sb = jnp.concatenate([k_ctx, k_loc[sb * W:(sb + 3) * W]], axis=0)
        v_sb = jnp.concatenate([v_ctx, v_loc[sb * W:(sb + 3) * W]], axis=0)
        pieces = [None] * A_HEADS
        for kvh in range(A_KV_HEADS):
            on_half = low if kvh == 0 else jnp.logical_not(low)
            stack, sinks = [], []
            for g in range(group):
                hd = kvh * group + g
                qh = q_ref[0, sb * W:(sb + 1) * W, (hd // 2) * LANES:(hd // 2 + 1) * LANES]
                if hd % 2 != kvh:
                    qh = pltpu.roll(qh, half, axis=1)
                stack.append(jnp.where(on_half, qh, 0.0))
                sinks.append(jnp.broadcast_to(sink_ref[hd:hd + 1, :] * LOG2_E, (W, LANES)))
            qs = jnp.concatenate(stack, axis=0).astype(BF16)
            sk = jnp.concatenate(sinks, axis=0)
            s = _dot_nt(qs, k_sb)
            s = jnp.concatenate([s[:, :lc], s[:, lc:] + bias], axis=1)
            m = jnp.maximum(jnp.max(s, axis=-1, keepdims=True), sk)
            e = jnp.exp2(s - jnp.concatenate([m] * (s.shape[1] // LANES), axis=1))
            denom = jnp.sum(e, axis=-1, keepdims=True) + jnp.exp2(sk - m)
            pv = _dot(e.astype(BF16), v_sb) / denom
            for g in range(group):
                hd = kvh * group + g
                piece = pv[g * W:(g + 1) * W]
                pieces[hd] = piece if hd % 2 == kvh else pltpu.roll(piece, half, axis=1)
        rows_out.append(jnp.concatenate(
            [jnp.where(low, pieces[2 * j], pieces[2 * j + 1]) for j in range(A_HEADS // 2)], axis=1))
    o_a = jnp.concatenate(rows_out, axis=0)

    y_in = jnp.concatenate([o_a * sga_ref[0], obg_ref[0]], axis=1).astype(BF16)
    y = _dot(y_in, w_ref[...])
    gate = mod_ref[0, 0, 2:3, :]
    o_ref[0] = _layer_norm(DEEPNORM_ALPHA * x_ref[0] + gate * y, g_ref[...], b_ref[...])


def _ab_attn(xcat, modsel, q, k, v, sga, obg, sink_b, w_out, ln_g, ln_b, ctx_len):
    B, L, _ = xcat.shape
    T = ROW_TILE
    W = WINDOW
    n_sub = T // W
    n_blk = L // W
    ctx_blk = ctx_len // W
    kern = functools.partial(_ab_attn_kernel, n_blk=n_blk, ctx_blk=ctx_blk)
    tok = lambda w: pl.BlockSpec((1, T, w), lambda b, i: (b, i, 0))
    prev = pl.BlockSpec((1, W, A_KV_WIDTH), lambda b, i: (b, jnp.maximum(i * n_sub - 1, 0), 0))
    nxt = pl.BlockSpec((1, W, A_KV_WIDTH), lambda b, i: (b, jnp.minimum((i + 1) * n_sub, n_blk - 1), 0))
    ctxs = pl.BlockSpec((1, ctx_len, A_KV_WIDTH), lambda b, i: (b, 0, 0))
    row = pl.BlockSpec((1, D_MODEL), lambda b, i: (0, 0))
    return pl.pallas_call(
        kern,
        grid=(B, L // T),
        in_specs=[
            tok(A_WIDTH),
            ctxs, prev, tok(A_KV_WIDTH), nxt,
            ctxs, prev, tok(A_KV_WIDTH), nxt,
            tok(A_WIDTH), tok(B_WIDTH), tok(D_MODEL),
            pl.BlockSpec((1, 1, 3, D_MODEL), lambda b, i: (b, (i * n_sub >= ctx_blk).astype(jnp.int32), 0, 0)),
            pl.BlockSpec((A_HEADS, LANES), lambda b, i: (0, 0)),
            pl.BlockSpec((D_MODEL, D_MODEL), lambda b, i: (0, 0)),
            row, row,
        ],
        out_specs=tok(D_MODEL),
        out_shape=jax.ShapeDtypeStruct((B, L, D_MODEL), F32),
        compiler_params=_params(2),
        name="ab_attn_out",
    )(q, k, k, k, k, v, v, v, v, sga, obg, xcat, modsel, sink_b, w_out, ln_g, ln_b)


def _forget_gate(z, lb):
    e = jnp.exp(-jnp.abs(z))
    return jnp.where(z >= 0, 1.0 + lb * e, lb + e) / (1.0 + e)


def _c_in_kernel(x_ref, mod_ref, w_ref, lb_ref, q_ref, ff_ref, fb_ref, v_ref, sg_ref):
    h = _modulate(x_ref[0], mod_ref).astype(BF16)

    def proj(idx):
        return _dot(h, w_ref[:, idx * C_WIDTH:(idx + 1) * C_WIDTH])

    def put(ref, val):
        for hd in range(C_HEADS):
            ref[0, hd] = val[:, hd * LANES:(hd + 1) * LANES]

    put(q_ref, _silu(proj(0)) * (C_KEY_DIM ** -0.5))
    put(v_ref, proj(3))
    sg_ref[0] = _silu(proj(4)).astype(sg_ref.dtype)
    for d, f_ref in enumerate((ff_ref, fb_ref)):
        put(f_ref, _forget_gate(proj(1 + d), jnp.clip(lb_ref[d:d + 1, :], 0.0, 1.0)))


def _c_in(xcat, modsel, w_in, lb, ctx_len):
    B, L, _ = xcat.shape
    tm = ROW_TILE
    n_tiles = L // tm
    ctx_tiles = ctx_len // tm
    width = w_in.shape[1]
    hm = pl.BlockSpec((1, C_HEADS, tm, LANES), lambda b, i: (b, 0, i, 0))
    hm_shape = jax.ShapeDtypeStruct((B, C_HEADS, L, LANES), F32)
    return pl.pallas_call(
        _c_in_kernel,
        grid=(B, n_tiles),
        in_specs=[
            pl.BlockSpec((1, tm, D_MODEL), lambda b, i: (b, i, 0)),
            pl.BlockSpec((1, 1, 3, D_MODEL), lambda b, i: (b, (i >= ctx_tiles).astype(jnp.int32), 0, 0)),
            pl.BlockSpec((D_MODEL, width), lambda b, i: (0, 0)),
            pl.BlockSpec((2, C_WIDTH), lambda b, i: (0, 0)),
        ],
        out_specs=[hm, hm, hm, hm, pl.BlockSpec((1, tm, C_WIDTH), lambda b, i: (b, i, 0))],
        out_shape=[hm_shape] * 4 + [jax.ShapeDtypeStruct((B, L, C_WIDTH), BF16)],
        compiler_params=_params(2),
        name="c_in",
    )(xcat, modsel, w_in, lb)


def _query_blocks(n, m, reverse):
    return [p for p in range(0, n, m) if ((p // m) % 2 == 1) != reverse]


def _level_step(q, k, run, rest, m, reverse, row):
    n = run.shape[0]
    if m >= SUBLANES:
        z, new_run, new_rest = [], [], []
        for p in range(0, n, 2 * m):
            lo, hi = slice(p, p + m), slice(p + m, p + 2 * m)
            first, second = (hi, lo) if reverse else (lo, hi)
            tot_first = run[p + m:p + m + 1] if reverse else run[p + m - 1:p + m]
            tot_second = run[p:p + 1] if reverse else run[p + 2 * m - 1:p + 2 * m]
            parts = {
                first: (k[first] * rest[first], run[first], rest[first] * tot_second),
                second: (q[second] * run[second], run[second] * tot_first, rest[second]),
            }
            for blk in (lo, hi):
                z.append(parts[blk][0])
                new_run.append(parts[blk][1])
                new_rest.append(parts[blk][2])
        return tuple(jnp.concatenate(t, axis=0) for t in (z, new_run, new_rest))
    grouped = lambda t: t.reshape(n // SUBLANES, SUBLANES, t.shape[1])
    up = lambda t, s: pltpu.roll(grouped(t), SUBLANES - s, axis=1).reshape(t.shape)
    down = lambda t, s: pltpu.roll(grouped(t), s, axis=1).reshape(t.shape)
    in_hi = row % (2 * m) >= m
    second = jnp.logical_not(in_hi) if reverse else in_hi
    if m == 1:
        own = run
    elif 2 * m < SUBLANES:
        own = jnp.where(row % 2 == 0, run, down(run, 1)) if reverse else jnp.where(row % 2 == 1, run, up(run, 1))
    if 2 * m < SUBLANES:
        other = jnp.where(in_hi, down(own, m), up(own, m))
    else:
        pieces = []
        for p in range(0, n, 2 * m):
            t_lo = run[p:p + 1] if reverse else run[p + m - 1:p + m]
            t_hi = run[p + m:p + m + 1] if reverse else run[p + 2 * m - 1:p + 2 * m]
            pieces.append(jnp.where(in_hi[p:p + 2 * m], t_lo, t_hi))
        other = jnp.concatenate(pieces, axis=0)
    if rest is None:
        z = jnp.where(second, q * run, k)
        new_rest = jnp.where(second, 1.0, other)
    else:
        z = jnp.where(second, q, k) * jnp.where(second, run, rest)
        new_rest = jnp.where(second, rest, rest * other)
    return z, jnp.where(second, run * other, run), new_rest


def _scan_operands(q, f, v, reverse):
    n = f.shape[0]
    k = 1.0 - f
    row = lax.broadcasted_iota(jnp.int32, f.shape, 0)
    run, rest = f, None
    levels = []
    for m in SCAN_LEVELS:
        z, run, rest = _level_step(q, k, run, rest, m, reverse, row)
        if m < SUBLANES:
            levels.append((None, z.astype(BF16)))
        else:
            zq = jnp.concatenate([z[p:p + m] for p in _query_blocks(n, m, reverse)], axis=0)
            levels.append((zq.astype(BF16), z.astype(BF16)))
    return dict(
        levels=levels,
        q_in=(q * run).astype(BF16),
        k_out=(k * rest).astype(BF16),
        own=jnp.sum(q * k, axis=-1, keepdims=True) * v,
        v=v.astype(BF16),
        v_t=v.T.astype(BF16),
        decay=run[0:1, :] if reverse else run[n - 1:n, :],
    )


def _scan_products(ops, state_t, mask_ref, d):
    n = ops["v"].shape[0]
    reverse = d == 1
    blocks = [None] * (n // SUBLANES)

    def add_rows(start, term):
        for r in range(0, term.shape[0], SUBLANES):
            b = (start + r) // SUBLANES
            piece = term[r:r + SUBLANES]
            blocks[b] = piece if blocks[b] is None else blocks[b] + piece

    for lvl, m in enumerate(SCAN_LEVELS):
        zq, z = ops["levels"][lvl]
        if zq is None:
            add_rows(0, _dot_nt(z, z) * mask_ref[d, lvl])
        else:
            starts = _query_blocks(n, m, reverse)
            mask = jnp.concatenate([mask_ref[d, lvl, p:p + m, :] for p in starts], axis=0)
            term = _dot_nt(zq, z) * mask
            for idx, p in enumerate(starts):
                add_rows(p, term[idx * m:(idx + 1) * m])
    weights = jnp.concatenate(blocks, axis=0).astype(BF16)
    o = _dot_nt(ops["q_in"], state_t.astype(BF16)) + _dot(weights, ops["v"]) + ops["own"]
    new_state = state_t * ops["decay"] + _dot(ops["v_t"], ops["k_out"])
    return o, new_state


def _c_scan_kernel(qf_ref, ff_ref, vf_ref, qb_ref, fb_ref, vb_ref, mask_ref, of_ref, ob_ref, st_ref):
    @pl.when(pl.program_id(1) == 0)
    def _():
        st_ref[...] = jnp.zeros(st_ref.shape, F32)

    dirs = ((qf_ref, ff_ref, vf_ref, of_ref), (qb_ref, fb_ref, vb_ref, ob_ref))
    units = [(hd, d) for hd in range(C_HEADS) for d in range(2)]

    def operands(unit):
        hd, d = unit
        q_r, f_r, v_r, _ = dirs[d]
        return _scan_operands(q_r[0, hd], f_r[0, hd], v_r[0, hd], d == 1)

    ops = operands(units[0])
    for idx, (hd, d) in enumerate(units):
        nxt = operands(units[idx + 1]) if idx + 1 < len(units) else None
        o, st = _scan_products(ops, st_ref[d, hd], mask_ref, d)
        dirs[d][3][0, hd] = o.astype(dirs[d][3].dtype)
        st_ref[d, hd] = st
        ops = nxt


def _scan_masks():
    n = SCAN_TILE
    c = np.arange(n)[:, None]
    s = np.arange(n)[None, :]
    fwd = []
    for m in SCAN_LEVELS:
        fwd.append(((c // (2 * m) == s // (2 * m)) & (c % (2 * m) >= m) & (s % (2 * m) < m)).astype(np.float32))
    fwd = np.stack(fwd)
    return jnp.asarray(np.stack([fwd, fwd.transpose(0, 2, 1)]))


def _c_scan(q, ff, fb, v, ctx_len):
    B, H, L, _ = q.shape
    T = SCAN_TILE
    n = L // T
    nc = ctx_len // T
    masks = _scan_masks()
    fwd = pl.BlockSpec((1, H, T, LANES), lambda b, j: (b, 0, j, 0))
    bwd = pl.BlockSpec((1, H, T, LANES), lambda b, j: (b, 0, jnp.where(j < nc, nc - 1 - j, n - 1 - (j - nc)), 0))
    hm_shape = jax.ShapeDtypeStruct((B, H, L, LANES), BF16)
    return pl.pallas_call(
        _c_scan_kernel,
        grid=(B, n),
        in_specs=[fwd, fwd, fwd, bwd, bwd, bwd,
                  pl.BlockSpec(masks.shape, lambda b, j: (0, 0, 0, 0))],
        out_specs=[fwd, bwd],
        out_shape=[hm_shape, hm_shape],
        scratch_shapes=[pltpu.VMEM((2, H, C_VAL_DIM, C_KEY_DIM), F32)],
        compiler_params=_params(2),
        name="c_scan",
    )(q, ff, v, q, fb, v, masks)


def _c_out_kernel(of_ref, ob_ref, sg_ref, x_ref, mod_ref, gn_ref, w_ref, g_ref, b_ref, o_ref):
    cols = []
    for hd in range(C_HEADS):
        o = of_ref[0, hd].astype(F32) + ob_ref[0, hd].astype(F32)
        ms = jnp.mean(o * o, axis=-1, keepdims=True)
        cols.append(o * lax.rsqrt(ms + RMS_EPS) * gn_ref[...])
    y_in = (jnp.concatenate(cols, axis=1) * sg_ref[0]).astype(BF16)
    y = _dot(y_in, w_ref[...])
    gate = mod_ref[0, 0, 2:3, :]
    o_ref[0] = _layer_norm(DEEPNORM_ALPHA * x_ref[0] + gate * y, g_ref[...], b_ref[...])


def _c_out(xcat, modsel, o_f, o_b, sg, gnorm, w_out, ln_g, ln_b, ctx_len, latent_only):
    B, L, _ = xcat.shape
    tm = ROW_TILE
    ctx_tiles = ctx_len // tm
    skip = ctx_tiles if latent_only else 0
    n_tiles = L // tm - skip
    hm = pl.BlockSpec((1, C_HEADS, tm, LANES), lambda b, i: (b, 0, i + skip, 0))
    tok = pl.BlockSpec((1, tm, D_MODEL), lambda b, i: (b, i + skip, 0))
    row = pl.BlockSpec((1, D_MODEL), lambda b, i: (0, 0))
    return pl.pallas_call(
        _c_out_kernel,
        grid=(B, n_tiles),
        in_specs=[
            hm, hm, tok, tok,
            pl.BlockSpec((1, 1, 3, D_MODEL), lambda b, i: (b, (i + skip >= ctx_tiles).astype(jnp.int32), 0, 0)),
            pl.BlockSpec((1, LANES), lambda b, i: (0, 0)),
            pl.BlockSpec((D_MODEL, D_MODEL), lambda b, i: (0, 0)),
            row, row,
        ],
        out_specs=pl.BlockSpec((1, tm, D_MODEL), lambda b, i: (b, i, 0)),
        out_shape=jax.ShapeDtypeStruct((B, n_tiles * tm, D_MODEL), F32),
        compiler_params=_params(2),
        name="c_out",
    )(o_f, o_b, sg, xcat, modsel, gnorm, w_out, ln_g, ln_b)


def _rope_tables(seq, ctx_len):
    t = jnp.arange(seq)
    freqs = ROPE_BASE ** (-jnp.arange(16, dtype=F32) / 16)
    ang_r = (t // GRID_W).astype(F32)[:, None] * freqs[None, :]
    ang_c = (t % GRID_W).astype(F32)[:, None] * freqs[None, :]
    cos = jnp.concatenate([jnp.cos(ang_r)] * 2 + [jnp.cos(ang_c)] * 2, axis=-1)
    sin = jnp.concatenate([-jnp.sin(ang_r), jnp.sin(ang_r), -jnp.sin(ang_c), jnp.sin(ang_c)], axis=-1)
    cos = jnp.concatenate([jnp.ones((ctx_len, 64), F32), cos], axis=0)
    sin = jnp.concatenate([jnp.zeros((ctx_len, 64), F32), sin], axis=0)
    return jnp.tile(cos, (1, 2)), jnp.tile(sin, (1, 2))


def kernel(x, c, ctx, c_ctx, w_ada, b_ada, ln_g, ln_b, w_in_ab, w_out_ab, sink_ab, conv_ab,
           w_in_c, w_out_c, lb_c, gnorm_c):
    B, seq, _ = x.shape
    ctx_len = ctx.shape[1]
    depth = w_ada.shape[0]
    assert seq % ROW_TILE == 0 and ctx_len % ROW_TILE == 0 and B <= 8

    s = jnp.concatenate([c, c_ctx[None, :], jnp.zeros((16 - B - 1, D_MODEL), F32)], axis=0)
    mod_all = _ada(s, w_ada, b_ada)
    mod_lat = mod_all[:, :B].reshape(depth, B, 1, 3, D_MODEL)
    mod_ctx = jnp.broadcast_to(mod_all[:, B].reshape(depth, 1, 1, 3, D_MODEL), mod_lat.shape)
    modsel = jnp.concatenate([mod_ctx, mod_lat], axis=2)

    lb_p = jax.nn.softmax(lb_c.astype(F32), axis=1)
    lb_all = jnp.cumsum(lb_p, axis=1) - lb_p[:, :1]

    cos_t, sin_t = _rope_tables(seq, ctx_len)
    xcat = jnp.concatenate([ctx, x], axis=1)
    for l in range(depth):
        j = l // 2
        g_row, b_row = ln_g[l][None, :], ln_b[l][None, :]
        if l % 2 == 0:
            q, k, v, sga, obg = _ab_in(xcat, modsel[l], w_in_ab[j].astype(BF16), conv_ab[j], cos_t, sin_t, ctx_len)
            sink_b = jnp.broadcast_to(sink_ab[j].astype(F32)[:, None], (A_HEADS, LANES))
            xcat = _ab_attn(xcat, modsel[l], q, k, v, sga, obg, sink_b, w_out_ab[j].astype(BF16),
                            g_row, b_row, ctx_len)
        else:
            q, ff, fb, v, sg = _c_in(xcat, modsel[l], w_in_c[j].astype(BF16), lb_all[:, j], ctx_len)
            o_f, o_b = _c_scan(q, ff, fb, v, ctx_len)
            xcat = _c_out(xcat, modsel[l], o_f, o_b, sg, gnorm_c[j][None, :], w_out_c[j].astype(BF16),
                          g_row, b_row, ctx_len, latent_only=(l == depth - 1))
    return xcat if depth % 2 == 0 else xcat[:, ctx_len:]
```

```python
import functools

import numpy as np
import jax
import jax.numpy as jnp
from jax import lax
from jax.experimental import pallas as pl
from jax.experimental.pallas import tpu as pltpu

F32 = jnp.float32
BF16 = jnp.bfloat16

D_MODEL = 1024
DEPTH = 4
GRID_W = 64
A_HEADS = 8
A_KV_HEADS = 2
A_HEAD_DIM = 64
A_WIDTH = A_HEADS * A_HEAD_DIM
A_KV_WIDTH = A_KV_HEADS * A_HEAD_DIM
WINDOW = 128
ROPE_BASE = 10000.0
B_WIDTH = D_MODEL // 2
C_HEADS = 8
C_KEY_DIM = 128
C_VAL_DIM = D_MODEL // C_HEADS
C_WIDTH = C_HEADS * C_KEY_DIM
DEEPNORM_ALPHA = (2 * DEPTH) ** 0.25
LN_EPS = 1e-5
RMS_EPS = 1e-6
LOG2_E = 1.4426950408889634

LANES = 128
SUBLANES = 8
ROW_TILE = 256
SCAN_TILE = 128
SCAN_LEVELS = (1, 2, 4, 8, 16, 32, 64)
HEAD_GROUP = 4
VMEM_LIMIT = 56 * 1024 * 1024


def _params(n_axes):
    return pltpu.CompilerParams(dimension_semantics=("arbitrary",) * n_axes, vmem_limit_bytes=VMEM_LIMIT)


def _dot(a, b):
    return jnp.dot(a, b, preferred_element_type=F32)


def _dot_nt(a, b):
    return lax.dot_general(a, b, (((1,), (1,)), ((), ())), preferred_element_type=F32)


def _silu(t):
    return t * jax.nn.sigmoid(t)


def _modulate(x, mod_ref):
    shift = mod_ref[0, 0, 0:1, :]
    scale = mod_ref[0, 0, 1:2, :]
    return x * (1 + scale) + shift


def _layer_norm(r, g, b):
    mu = jnp.mean(r, axis=-1, keepdims=True)
    rc = r - mu
    var = jnp.mean(rc * rc, axis=-1, keepdims=True)
    return rc * lax.rsqrt(var + LN_EPS) * g + b


def _ada_kernel(s_ref, w_ref, b_ref, o_ref):
    s = _silu(s_ref[...]).astype(BF16)
    o_ref[0] = _dot(s, w_ref[0].astype(BF16)) + b_ref[0]


def _ada(s, w_ada, b_ada):
    depth = w_ada.shape[0]
    rows = s.shape[0]
    n_col = 3 * D_MODEL // D_MODEL
    return pl.pallas_call(
        _ada_kernel,
        grid=(depth, n_col),
        in_specs=[
            pl.BlockSpec((rows, D_MODEL), lambda l, n: (0, 0)),
            pl.BlockSpec((1, D_MODEL, D_MODEL), lambda l, n: (l, 0, n)),
            pl.BlockSpec((1, 1, D_MODEL), lambda l, n: (l, 0, n)),
        ],
        out_specs=pl.BlockSpec((1, rows, D_MODEL), lambda l, n: (l, 0, n)),
        out_shape=jax.ShapeDtypeStruct((depth, rows, 3 * D_MODEL), F32),
        compiler_params=_params(2),
        name="ada_mod",
    )(s, w_ada, b_ada.reshape(depth, 1, 3 * D_MODEL))


def _rope(t, cos, sin_signed, lane_lo):
    partner = jnp.where(lane_lo, pltpu.roll(t, LANES - 16, axis=1), pltpu.roll(t, 16, axis=1))
    return t * cos + partner * sin_signed


def _ab_in_kernel(x_ref, xp_ref, xn_ref, mod_ref, w_ref, cw_ref, cos_ref, sin_ref,
                  q_ref, k_ref, v_ref, sga_ref, obg_ref, *, n_tiles, ctx_tiles):
    i = pl.program_id(1)
    tm = x_ref.shape[1]
    h = _modulate(x_ref[0], mod_ref).astype(BF16)

    def proj(lhs, lo, hi):
        return _dot(lhs, w_ref[:, lo:hi])

    o_q, o_k, o_v, o_ga = 0, A_WIDTH, A_WIDTH + A_KV_WIDTH, A_WIDTH + 2 * A_KV_WIDTH
    o_xb = o_ga + A_WIDTH
    o_bg, o_cg, o_gb = o_xb + B_WIDTH, o_xb + 2 * B_WIDTH, o_xb + 3 * B_WIDTH

    cos = cos_ref[...]
    sin_signed = sin_ref[...]
    lane = lax.broadcasted_iota(jnp.int32, (tm, LANES), 1)
    lane_lo = (lane % 32) < 16
    q = proj(h, o_q, o_k)
    for j in range(A_WIDTH // LANES):
        qj = _rope(q[:, j * LANES:(j + 1) * LANES], cos, sin_signed, lane_lo)
        q_ref[0, :, j * LANES:(j + 1) * LANES] = qj * (A_HEAD_DIM ** -0.5 * LOG2_E)
    k_ref[0] = _rope(proj(h, o_k, o_v), cos, sin_signed, lane_lo)
    v_ref[0] = proj(h, o_v, o_ga)
    sga_ref[0] = _silu(proj(h, o_ga, o_xb))

    u = proj(h, o_cg, o_gb) * proj(h, o_xb, o_bg)
    halo = jnp.concatenate([xp_ref[0], xn_ref[0]], axis=0)
    hh = _modulate(halo, mod_ref).astype(BF16)
    uh = proj(hh, o_cg, o_gb) * proj(hh, o_xb, o_bg)
    left_ok = jnp.logical_and(i != 0, i != ctx_tiles)
    right_ok = jnp.logical_and(i != ctx_tiles - 1, i != n_tiles - 1)
    u_left = jnp.where(left_ok, uh[SUBLANES - 1:SUBLANES], 0.0)
    u_right = jnp.where(right_ok, uh[SUBLANES:SUBLANES + 1], 0.0)
    row = lax.broadcasted_iota(jnp.int32, u.shape, 0)
    u_prev = jnp.where(row == 0, u_left, pltpu.roll(u, 1, axis=0))
    u_next = jnp.where(row == tm - 1, u_right, pltpu.roll(u, tm - 1, axis=0))
    conv = u_prev * cw_ref[0:1, :] + u * cw_ref[1:2, :] + u_next * cw_ref[2:3, :]
    obg_ref[0] = proj(h, o_bg, o_cg) * conv * _silu(proj(h, o_gb, o_gb + B_WIDTH))


def _ab_in(xcat, modsel, w_in, conv_w, cos_t, sin_t, ctx_len):
    B, L, _ = xcat.shape
    tm = ROW_TILE
    n_tiles = L // tm
    ctx_tiles = ctx_len // tm
    halo_blocks = tm // SUBLANES
    n_halo = L // SUBLANES
    kern = functools.partial(_ab_in_kernel, n_tiles=n_tiles, ctx_tiles=ctx_tiles)
    width = w_in.shape[1]
    tok = lambda w: pl.BlockSpec((1, tm, w), lambda b, i: (b, i, 0))
    return pl.pallas_call(
        kern,
        grid=(B, n_tiles),
        in_specs=[
            tok(D_MODEL),
            pl.BlockSpec((1, SUBLANES, D_MODEL), lambda b, i: (b, jnp.maximum(i * halo_blocks - 1, 0), 0)),
            pl.BlockSpec((1, SUBLANES, D_MODEL), lambda b, i: (b, jnp.minimum((i + 1) * halo_blocks, n_halo - 1), 0)),
            pl.BlockSpec((1, 1, 3, D_MODEL), lambda b, i: (b, (i >= ctx_tiles).astype(jnp.int32), 0, 0)),
            pl.BlockSpec((D_MODEL, width), lambda b, i: (0, 0)),
            pl.BlockSpec((3, B_WIDTH), lambda b, i: (0, 0)),
            pl.BlockSpec((tm, LANES), lambda b, i: (i, 0)),
            pl.BlockSpec((tm, LANES), lambda b, i: (i, 0)),
        ],
        out_specs=[tok(A_WIDTH), tok(A_KV_WIDTH), tok(A_KV_WIDTH), tok(A_WIDTH), tok(B_WIDTH)],
        out_shape=[
            jax.ShapeDtypeStruct((B, L, A_WIDTH), F32),
            jax.ShapeDtypeStruct((B, L, A_KV_WIDTH), F32),
            jax.ShapeDtypeStruct((B, L, A_KV_WIDTH), F32),
            jax.ShapeDtypeStruct((B, L, A_WIDTH), F32),
            jax.ShapeDtypeStruct((B, L, B_WIDTH), F32),
        ],
        compiler_params=_params(2),
        name="ab_in",
    )(xcat, xcat, xcat, modsel, w_in, conv_w, cos_t, sin_t)


def _ab_attn_kernel(q_ref, kx_ref, kp_ref, kc_ref, kn_ref, vx_ref, vp_ref, vc_ref, vn_ref,
                    sga_ref, obg_ref, x_ref, mod_ref, sink_ref, w_ref, g_ref, b_ref, o_ref,
                    *, n_blk, ctx_blk):
    i = pl.program_id(1)
    W = WINDOW
    n_sub = q_ref.shape[1] // W
    lc = kx_ref.shape[1]
    half = LANES // 2
    group = A_HEADS // A_KV_HEADS

    k_ctx, v_ctx = kx_ref[0].astype(BF16), vx_ref[0].astype(BF16)
    k_loc = jnp.concatenate([kp_ref[0], kc_ref[0], kn_ref[0]], axis=0).astype(BF16)
    v_loc = jnp.concatenate([vp_ref[0], vc_ref[0], vn_ref[0]], axis=0).astype(BF16)

    is_lat = i * n_sub >= ctx_blk
    ri = lax.broadcasted_iota(jnp.int32, (W, W), 0)
    ci = lax.broadcasted_iota(jnp.int32, (W, W), 1)
    neg = jnp.float32(-jnp.inf)
    lane = lax.broadcasted_iota(jnp.int32, (W, LANES), 1)
    low = lane < half

    rows_out = []
    for sb in range(n_sub):
        blk = i * n_sub + sb
        prev_ok = jnp.logical_and(is_lat, blk - 1 >= ctx_blk)
        next_ok = jnp.logical_and(is_lat, blk + 1 <= n_blk - 1)
        bias = jnp.concatenate([
            jnp.where(jnp.logical_and(prev_ok, ci >= ri), 0.0, neg),
            jnp.where(jnp.logical_and(is_lat, ci >= 0), 0.0, neg),
            jnp.where(jnp.logical_and(next_ok, ci <= ri), 0.0, neg),
        ], axis=1)
        bias = jnp.concatenate([bias] * group, axis=0)
        k_sb = jnp.concatenate([k_ctx, k_loc[sb * W:(sb + 3) * W]], axis=0)
        v_sb = jnp.concatenate([v_ctx, v_loc[sb * W:(sb + 3) * W]], axis=0)
        pieces = [None] * A_HEADS
        for kvh in range(A_KV_HEADS):
            on_half = low if kvh == 0 else jnp.logical_not(low)
            stack, sinks = [], []
            for g in range(group):
                hd = kvh * group + g
                qh = q_ref[0, sb * W:(sb + 1) * W, (hd // 2) * LANES:(hd // 2 + 1) * LANES]
                if hd % 2 != kvh:
                    qh = pltpu.roll(qh, half, axis=1)
                stack.append(jnp.where(on_half, qh, 0.0))
                sinks.append(jnp.broadcast_to(sink_ref[hd:hd + 1, :] * LOG2_E, (W, LANES)))
            qs = jnp.concatenate(stack, axis=0).astype(BF16)
            sk = jnp.concatenate(sinks, axis=0)
            s = _dot_nt(qs, k_sb)
            s = jnp.concatenate([s[:, :lc], s[:, lc:] + bias], axis=1)
            m = jnp.maximum(jnp.max(s, axis=-1, keepdims=True), sk)
            e = jnp.exp2(s - jnp.concatenate([m] * (s.shape[1] // LANES), axis=1))
            denom = jnp.sum(e, axis=-1, keepdims=True) + jnp.exp2(sk - m)
            pv = _dot(e.astype(BF16), v_sb) / denom
            for g in range(group):
                hd = kvh * group + g
                piece = pv[g * W:(g + 1) * W]
                pieces[hd] = piece if hd % 2 == kvh else pltpu.roll(piece, half, axis=1)
        rows_out.append(jnp.concatenate(
            [jnp.where(low, pieces[2 * j], pieces[2 * j + 1]) for j in range(A_HEADS // 2)], axis=1))
    o_a = jnp.concatenate(rows_out, axis=0)

    y_in = jnp.concatenate([o_a * sga_ref[0], obg_ref[0]], axis=1).astype(BF16)
    y = _dot(y_in, w_ref[...])
    gate = mod_ref[0, 0, 2:3, :]
    o_ref[0] = _layer_norm(DEEPNORM_ALPHA * x_ref[0] + gate * y, g_ref[...], b_ref[...])


def _ab_attn(xcat, modsel, q, k, v, sga, obg, sink_b, w_out, ln_g, ln_b, ctx_len):
    B, L, _ = xcat.shape
    T = ROW_TILE
    W = WINDOW
    n_sub = T // W
    n_blk = L // W
    ctx_blk = ctx_len // W
    kern = functools.partial(_ab_attn_kernel, n_blk=n_blk, ctx_blk=ctx_blk)
    tok = lambda w: pl.BlockSpec((1, T, w), lambda b, i: (b, i, 0))
    prev = pl.BlockSpec((1, W, A_KV_WIDTH), lambda b, i: (b, jnp.maximum(i * n_sub - 1, 0), 0))
    nxt = pl.BlockSpec((1, W, A_KV_WIDTH), lambda b, i: (b, jnp.minimum((i + 1) * n_sub, n_blk - 1), 0))
    ctxs = pl.BlockSpec((1, ctx_len, A_KV_WIDTH), lambda b, i: (b, 0, 0))
    row = pl.BlockSpec((1, D_MODEL), lambda b, i: (0, 0))
    return pl.pallas_call(
        kern,
        grid=(B, L // T),
        in_specs=[
            tok(A_WIDTH),
            ctxs, prev, tok(A_KV_WIDTH), nxt,
            ctxs, prev, tok(A_KV_WIDTH), nxt,
            tok(A_WIDTH), tok(B_WIDTH), tok(D_MODEL),
            pl.BlockSpec((1, 1, 3, D_MODEL), lambda b, i: (b, (i * n_sub >= ctx_blk).astype(jnp.int32), 0, 0)),
            pl.BlockSpec((A_HEADS, LANES), lambda b, i: (0, 0)),
            pl.BlockSpec((D_MODEL, D_MODEL), lambda b, i: (0, 0)),
            row, row,
        ],
        out_specs=tok(D_MODEL),
        out_shape=jax.ShapeDtypeStruct((B, L, D_MODEL), F32),
        compiler_params=_params(2),
        name="ab_attn_out",
    )(q, k, k, k, k, v, v, v, v, sga, obg, xcat, modsel, sink_b, w_out, ln_g, ln_b)


def _forget_gate(z, lb):
    e = jnp.exp(-jnp.abs(z))
    return jnp.where(z >= 0, 1.0 + lb * e, lb + e) / (1.0 + e)


def _c_in_kernel(x_ref, mod_ref, w_ref, lb_ref, q_ref, ff_ref, fb_ref, v_ref, sg_ref):
    h = _modulate(x_ref[0], mod_ref).astype(BF16)

    def proj(idx):
        return _dot(h, w_ref[:, idx * C_WIDTH:(idx + 1) * C_WIDTH])

    def put(ref, val):
        tm = val.shape[0]
        for hd in range(C_HEADS):
            g, hh = divmod(hd, HEAD_GROUP)
            ref[0, g, pl.ds(hh, tm, stride=HEAD_GROUP), :] = val[:, hd * LANES:(hd + 1) * LANES]

    put(q_ref, _silu(proj(0)) * (C_KEY_DIM ** -0.5))
    put(v_ref, proj(3))
    sg_ref[0] = _silu(proj(4)).astype(sg_ref.dtype)
    for d, f_ref in enumerate((ff_ref, fb_ref)):
        put(f_ref, _forget_gate(proj(1 + d), jnp.clip(lb_ref[d:d + 1, :], 0.0, 1.0)))


def _c_in(xcat, modsel, w_in, lb, ctx_len):
    B, L, _ = xcat.shape
    tm = ROW_TILE
    n_tiles = L // tm
    ctx_tiles = ctx_len // tm
    width = w_in.shape[1]
    groups = C_HEADS // HEAD_GROUP
    hm = pl.BlockSpec((1, groups, tm * HEAD_GROUP, LANES), lambda b, i: (b, 0, i, 0))
    hm_shape = jax.ShapeDtypeStruct((B, groups, L * HEAD_GROUP, LANES), F32)
    return pl.pallas_call(
        _c_in_kernel,
        grid=(B, n_tiles),
        in_specs=[
            pl.BlockSpec((1, tm, D_MODEL), lambda b, i: (b, i, 0)),
            pl.BlockSpec((1, 1, 3, D_MODEL), lambda b, i: (b, (i >= ctx_tiles).astype(jnp.int32), 0, 0)),
            pl.BlockSpec((D_MODEL, width), lambda b, i: (0, 0)),
            pl.BlockSpec((2, C_WIDTH), lambda b, i: (0, 0)),
        ],
        out_specs=[hm, hm, hm, hm, pl.BlockSpec((1, tm, C_WIDTH), lambda b, i: (b, i, 0))],
        out_shape=[hm_shape] * 4 + [jax.ShapeDtypeStruct((B, L, C_WIDTH), BF16)],
        compiler_params=_params(2),
        name="c_in",
    )(xcat, modsel, w_in, lb)


def _query_blocks(n, m, reverse):
    return [p for p in range(0, n, m) if ((p // m) % 2 == 1) != reverse]


N_SCAN_OPERANDS = len(SCAN_LEVELS) + 3


SCAN_CHUNK = 16
LOOKAHEAD = 2


def _scan_elementwise(q_ref, f_ref, v_ref, reverse, out_ref, decay_ref):
    assert SUBLANES == 2 * HEAD_GROUP
    n_rows, width = q_ref.shape
    chunk_rows = SCAN_CHUNK * HEAD_GROUP
    grouped = lambda t: t.reshape(t.shape[0] // SUBLANES, SUBLANES, width)
    low1 = lax.broadcasted_iota(jnp.int32, (1, SUBLANES, width), 1) < HEAD_GROUP
    swap = lambda t: pltpu.roll(t, HEAD_GROUP, axis=1)

    def block_total(t):
        return jnp.where(low1, t, swap(t)) if reverse else jnp.where(low1, swap(t), t)

    def double(q, k, run, rest, g):
        z, new_run, new_rest = [], [], []
        for a in range(0, run.shape[0], 2 * g):
            lo, hi = slice(a, a + g), slice(a + g, a + 2 * g)
            first, last = (hi, lo) if reverse else (lo, hi)
            edge = (lambda blk: run[blk.start:blk.start + 1]) if reverse else (lambda blk: run[blk.stop - 1:blk.stop])
            parts = {
                first: (k[first] * rest[first], run[first], rest[first] * block_total(edge(last))),
                last: (q[last] * run[last], run[last] * block_total(edge(first)), rest[last]),
            }
            for blk in (lo, hi):
                z.append(parts[blk][0])
                new_run.append(parts[blk][1])
                new_rest.append(parts[blk][2])
        return tuple(jnp.concatenate(t, axis=0) for t in (z, new_run, new_rest))

    def put(idx, rows, t):
        out_ref[idx, rows, :] = t.reshape(t.shape[0] * SUBLANES, width)

    low_levels = [m for m in SCAN_LEVELS if m < SCAN_CHUNK]
    n_low, n_lvl = len(low_levels), len(SCAN_LEVELS)
    chunks = [slice(r0, r0 + chunk_rows) for r0 in range(0, n_rows, chunk_rows)]
    q_run, k_rest, total = [], [], []
    for rows in chunks:
        q, f, v = grouped(q_ref[rows, :]), grouped(f_ref[rows, :]), grouped(v_ref[rows, :])
        k = 1.0 - f
        put(n_lvl + 2, rows, jnp.sum(q * k, axis=-1, keepdims=True) * v)
        second = low1 if reverse else jnp.logical_not(low1)
        other = swap(f)
        put(0, rows, jnp.where(second, q * f, k))
        run = jnp.where(second, f * other, f)
        rest = jnp.where(second, 1.0, other)
        for lvl, m in enumerate(low_levels[1:], start=1):
            z, run, rest = double(q, k, run, rest, m // 2)
            put(lvl, rows, z)
        q_run.append(q * run)
        k_rest.append(k * rest)
        total.append(block_total(run[0:1] if reverse else run[run.shape[0] - 1:]))

    mul = lambda a, b: b if a is None else a * b
    before, after = [None] * len(chunks), [None] * len(chunks)
    for lvl, m in enumerate(SCAN_LEVELS[n_low:], start=n_low):
        g = m // SCAN_CHUNK
        new_total = []
        for a in range(0, len(chunks), 2 * g):
            lo, hi = list(range(a, a + g)), list(range(a + g, a + 2 * g))
            first, last = (hi, lo) if reverse else (lo, hi)
            t_first, t_last = total[first[0] // g], total[last[0] // g]
            for c in first:
                put(lvl, chunks[c], mul(after[c], k_rest[c]))
                after[c] = mul(after[c], t_last)
            for c in last:
                put(lvl, chunks[c], mul(before[c], q_run[c]))
                before[c] = mul(before[c], t_first)
            new_total.append(t_first * t_last)
        total = new_total
    for c, rows in enumerate(chunks):
        put(n_lvl, rows, mul(before[c], q_run[c]))
        put(n_lvl + 1, rows, mul(after[c], k_rest[c]))
    decay_ref[...] = total[0][0]


def _scan_products(load, v, decay, state_t, mask_ref, d):
    n = v.shape[0]
    reverse = d == 1
    n_lvl = len(SCAN_LEVELS)
    blocks = [None] * (n // SUBLANES)

    def add_rows(start, term):
        for r in range(0, term.shape[0], SUBLANES):
            b = (start + r) // SUBLANES
            piece = term[r:r + SUBLANES]
            blocks[b] = piece if blocks[b] is None else blocks[b] + piece

    for lvl, m in enumerate(SCAN_LEVELS):
        z = load(lvl)
        if m < SUBLANES:
            zb = z.astype(BF16)
            add_rows(0, _dot_nt(zb, zb) * mask_ref[d, lvl])
        else:
            starts = _query_blocks(n, m, reverse)
            zq = jnp.concatenate([z[p:p + m] for p in starts], axis=0).astype(BF16)
            mask = jnp.concatenate([mask_ref[d, lvl, p:p + m, :] for p in starts], axis=0)
            term = _dot_nt(zq, z.astype(BF16)) * mask
            for idx, p in enumerate(starts):
                add_rows(p, term[idx * m:(idx + 1) * m])
    weights = jnp.concatenate(blocks, axis=0).astype(BF16)
    q_in, k_out, own = load(n_lvl).astype(BF16), load(n_lvl + 1).astype(BF16), load(n_lvl + 2)
    o_partial = _dot_nt(q_in, state_t.astype(BF16)) + own
    new_state = state_t * decay + _dot(v.T.astype(BF16), k_out)
    return weights, v.astype(BF16), o_partial, new_state


def _c_scan_kernel(qf_ref, ff_ref, vf_ref, qb_ref, fb_ref, vb_ref, mask_ref, of_ref, ob_ref,
                   st_ref, op0_ref, op1_ref, decay0_ref, decay1_ref):
    @pl.when(pl.program_id(1) == 0)
    def _():
        st_ref[...] = jnp.zeros(st_ref.shape, F32)

    T = SCAN_TILE
    dirs = ((qf_ref, ff_ref, vf_ref, of_ref), (qb_ref, fb_ref, vb_ref, ob_ref))
    units = [(g, d) for g in range(C_HEADS // HEAD_GROUP) for d in range(2)]
    op_refs, decay_refs = (op0_ref, op1_ref), (decay0_ref, decay1_ref)

    def elementwise(idx):
        g, d = units[idx]
        q_r, f_r, v_r, _ = dirs[d]
        slot = idx % 2
        _scan_elementwise(q_r.at[0, g], f_r.at[0, g], v_r.at[0, g], d == 1, op_refs[slot], decay_refs[slot])

    def finish(job):
        (d, hd), (weights, v_bf, o_partial, _) = job
        o_r = dirs[d][3]
        o_r[0, hd] = (o_partial + _dot(weights, v_bf)).astype(o_r.dtype)

    elementwise(0)
    pending = []
    for idx, (g, d) in enumerate(units):
        if idx + 1 < len(units):
            elementwise(idx + 1)
        slot = idx % 2
        for hh in range(HEAD_GROUP):
            hd = g * HEAD_GROUP + hh
            rows = pl.ds(hh, T, stride=HEAD_GROUP)
            load = lambda i, rows=rows, slot=slot: op_refs[slot][i, rows, :]
            job = ((d, hd), _scan_products(load, dirs[d][2][0, g, rows, :], decay_refs[slot][hh:hh + 1, :],
                                           st_ref[d, hd], mask_ref, d))
            st_ref[d, hd] = job[1][3]
            pending.append(job)
            if len(pending) > LOOKAHEAD:
                finish(pending.pop(0))
    for job in pending:
        finish(job)


def _scan_masks():
    n = SCAN_TILE
    c = np.arange(n)[:, None]
    s = np.arange(n)[None, :]
    fwd = []
    for m in SCAN_LEVELS:
        fwd.append(((c // (2 * m) == s // (2 * m)) & (c % (2 * m) >= m) & (s % (2 * m) < m)).astype(np.float32))
    fwd = np.stack(fwd)
    return jnp.asarray(np.stack([fwd, fwd.transpose(0, 2, 1)]))


def _c_scan(q, ff, fb, v, ctx_len):
    B, groups, rows, _ = q.shape
    H = groups * HEAD_GROUP
    L = rows // HEAD_GROUP
    T = SCAN_TILE
    n = L // T
    nc = ctx_len // T
    masks = _scan_masks()
    back = lambda j: jnp.where(j < nc, nc - 1 - j, n - 1 - (j - nc))
    fwd_in = pl.BlockSpec((1, groups, T * HEAD_GROUP, LANES), lambda b, j: (b, 0, j, 0))
    bwd_in = pl.BlockSpec((1, groups, T * HEAD_GROUP, LANES), lambda b, j: (b, 0, back(j), 0))
    fwd = pl.BlockSpec((1, H, T, LANES), lambda b, j: (b, 0, j, 0))
    bwd = pl.BlockSpec((1, H, T, LANES), lambda b, j: (b, 0, back(j), 0))
    hm_shape = jax.ShapeDtypeStruct((B, H, L, LANES), BF16)
    return pl.pallas_call(
        _c_scan_kernel,
        grid=(B, n),
        in_specs=[fwd_in, fwd_in, fwd_in, bwd_in, bwd_in, bwd_in,
                  pl.BlockSpec(masks.shape, lambda b, j: (0, 0, 0, 0))],
        out_specs=[fwd, bwd],
        out_shape=[hm_shape, hm_shape],
        scratch_shapes=[pltpu.VMEM((2, H, C_VAL_DIM, C_KEY_DIM), F32),
                        pltpu.VMEM((N_SCAN_OPERANDS, T * HEAD_GROUP, LANES), F32),
                        pltpu.VMEM((N_SCAN_OPERANDS, T * HEAD_GROUP, LANES), F32),
                        pltpu.VMEM((SUBLANES, LANES), F32),
                        pltpu.VMEM((SUBLANES, LANES), F32)],
        compiler_params=_params(2),
        name="c_scan",
    )(q, ff, v, q, fb, v, masks)


def _c_out_kernel(of_ref, ob_ref, sg_ref, x_ref, mod_ref, gn_ref, w_ref, g_ref, b_ref, o_ref):
    cols = []
    for hd in range(C_HEADS):
        o = of_ref[0, hd].astype(F32) + ob_ref[0, hd].astype(F32)
        ms = jnp.mean(o * o, axis=-1, keepdims=True)
        cols.append(o * lax.rsqrt(ms + RMS_EPS) * gn_ref[...])
    y_in = (jnp.concatenate(cols, axis=1) * sg_ref[0]).astype(BF16)
    y = _dot(y_in, w_ref[...])
    gate = mod_ref[0, 0, 2:3, :]
    o_ref[0] = _layer_norm(DEEPNORM_ALPHA * x_ref[0] + gate * y, g_ref[...], b_ref[...])


def _c_out(xcat, modsel, o_f, o_b, sg, gnorm, w_out, ln_g, ln_b, ctx_len, latent_only):
    B, L, _ = xcat.shape
    tm = ROW_TILE
    ctx_tiles = ctx_len // tm
    skip = ctx_tiles if latent_only else 0
    n_tiles = L // tm - skip
    hm = pl.BlockSpec((1, C_HEADS, tm, LANES), lambda b, i: (b, 0, i + skip, 0))
    tok = pl.BlockSpec((1, tm, D_MODEL), lambda b, i: (b, i + skip, 0))
    row = pl.BlockSpec((1, D_MODEL), lambda b, i: (0, 0))
    return pl.pallas_call(
        _c_out_kernel,
        grid=(B, n_tiles),
        in_specs=[
            hm, hm, tok, tok,
            pl.BlockSpec((1, 1, 3, D_MODEL), lambda b, i: (b, (i + skip >= ctx_tiles).astype(jnp.int32), 0, 0)),
            pl.BlockSpec((1, LANES), lambda b, i: (0, 0)),
            pl.BlockSpec((D_MODEL, D_MODEL), lambda b, i: (0, 0)),
            row, row,
        ],
        out_specs=pl.BlockSpec((1, tm, D_MODEL), lambda b, i: (b, i, 0)),
        out_shape=jax.ShapeDtypeStruct((B, n_tiles * tm, D_MODEL), F32),
        compiler_params=_params(2),
        name="c_out",
    )(o_f, o_b, sg, xcat, modsel, gnorm, w_out, ln_g, ln_b)


def _rope_tables(seq, ctx_len):
    t = jnp.arange(seq)
    freqs = ROPE_BASE ** (-jnp.arange(16, dtype=F32) / 16)
    ang_r = (t // GRID_W).astype(F32)[:, None] * freqs[None, :]
    ang_c = (t % GRID_W).astype(F32)[:, None] * freqs[None, :]
    cos = jnp.concatenate([jnp.cos(ang_r)] * 2 + [jnp.cos(ang_c)] * 2, axis=-1)
    sin = jnp.concatenate([-jnp.sin(ang_r), jnp.sin(ang_r), -jnp.sin(ang_c), jnp.sin(ang_c)], axis=-1)
    cos = jnp.concatenate([jnp.ones((ctx_len, 64), F32), cos], axis=0)
    sin = jnp.concatenate([jnp.zeros((ctx_len, 64), F32), sin], axis=0)
    return jnp.tile(cos, (1, 2)), jnp.tile(sin, (1, 2))


def kernel(x, c, ctx, c_ctx, w_ada, b_ada, ln_g, ln_b, w_in_ab, w_out_ab, sink_ab, conv_ab,
           w_in_c, w_out_c, lb_c, gnorm_c):
    B, seq, _ = x.shape
    ctx_len = ctx.shape[1]
    depth = w_ada.shape[0]
    assert seq % ROW_TILE == 0 and ctx_len % ROW_TILE == 0 and B <= 8

    s = jnp.concatenate([c, c_ctx[None, :], jnp.zeros((16 - B - 1, D_MODEL), F32)], axis=0)
    mod_all = _ada(s, w_ada, b_ada)
    mod_lat = mod_all[:, :B].reshape(depth, B, 1, 3, D_MODEL)
    mod_ctx = jnp.broadcast_to(mod_all[:, B].reshape(depth, 1, 1, 3, D_MODEL), mod_lat.shape)
    modsel = jnp.concatenate([mod_ctx, mod_lat], axis=2)

    lb_p = jax.nn.softmax(lb_c.astype(F32), axis=1)
    lb_all = jnp.cumsum(lb_p, axis=1) - lb_p[:, :1]

    cos_t, sin_t = _rope_tables(seq, ctx_len)
    xcat = jnp.concatenate([ctx, x], axis=1)
    for l in range(depth):
        j = l // 2
        g_row, b_row = ln_g[l][None, :], ln_b[l][None, :]
        if l % 2 == 0:
            q, k, v, sga, obg = _ab_in(xcat, modsel[l], w_in_ab[j].astype(BF16), conv_ab[j], cos_t, sin_t, ctx_len)
            sink_b = jnp.broadcast_to(sink_ab[j].astype(F32)[:, None], (A_HEADS, LANES))
            xcat = _ab_attn(xcat, modsel[l], q, k, v, sga, obg, sink_b, w_out_ab[j].astype(BF16),
                            g_row, b_row, ctx_len)
        else:
            q, ff, fb, v, sg = _c_in(xcat, modsel[l], w_in_c[j].astype(BF16), lb_all[:, j], ctx_len)
            o_f, o_b = _c_scan(q, ff, fb, v, ctx_len)
            xcat = _c_out(xcat, modsel[l], o_f, o_b, sg, gnorm_c[j][None, :], w_out_c[j].astype(BF16),
                          g_row, b_row, ctx_len, latent_only=(l == depth - 1))
    return xcat if depth % 2 == 0 else xcat[:, ctx_len:]
```

```python
import functools

import numpy as np
import jax
import jax.numpy as jnp
from jax import lax
from jax.experimental import pallas as pl
from jax.experimental.pallas import tpu as pltpu

F32 = jnp.float32
BF16 = jnp.bfloat16

D_MODEL = 1024
DEPTH = 4
GRID_W = 64
A_HEADS = 8
A_KV_HEADS = 2
A_HEAD_DIM = 64
A_WIDTH = A_HEADS * A_HEAD_DIM
A_KV_WIDTH = A_KV_HEADS * A_HEAD_DIM
WINDOW = 128
ROPE_BASE = 10000.0
B_WIDTH = D_MODEL // 2
C_HEADS = 8
C_KEY_DIM = 128
C_VAL_DIM = D_MODEL // C_HEADS
C_WIDTH = C_HEADS * C_KEY_DIM
DEEPNORM_ALPHA = (2 * DEPTH) ** 0.25
LN_EPS = 1e-5
RMS_EPS = 1e-6
LOG2_E = 1.4426950408889634

LANES = 128
SUBLANES = 8
ROW_TILE = 256
SCAN_TILE = 128
SCAN_LEVELS = (1, 2, 4, 8, 16, 32, 64)
HEAD_GROUP = 4
VMEM_LIMIT = 56 * 1024 * 1024


def _params(n_axes):
    return pltpu.CompilerParams(dimension_semantics=("arbitrary",) * n_axes, vmem_limit_bytes=VMEM_LIMIT)


def _dot(a, b):
    return jnp.dot(a, b, preferred_element_type=F32)


def _dot_nt(a, b):
    return lax.dot_general(a, b, (((1,), (1,)), ((), ())), preferred_element_type=F32)


def _silu(t):
    return t * jax.nn.sigmoid(t)


def _modulate(x, mod_ref):
    shift = mod_ref[0, 0, 0:1, :]
    scale = mod_ref[0, 0, 1:2, :]
    return x * (1 + scale) + shift


def _layer_norm(r, g, b):
    mu = jnp.mean(r, axis=-1, keepdims=True)
    rc = r - mu
    var = jnp.mean(rc * rc, axis=-1, keepdims=True)
    return rc * lax.rsqrt(var + LN_EPS) * g + b


def _ada_kernel(s_ref, w_ref, b_ref, o_ref):
    s = _silu(s_ref[...]).astype(BF16)
    o_ref[0] = _dot(s, w_ref[0].astype(BF16)) + b_ref[0]


def _ada(s, w_ada, b_ada):
    depth = w_ada.shape[0]
    rows = s.shape[0]
    n_col = 3 * D_MODEL // D_MODEL
    return pl.pallas_call(
        _ada_kernel,
        grid=(depth, n_col),
        in_specs=[
            pl.BlockSpec((rows, D_MODEL), lambda l, n: (0, 0)),
            pl.BlockSpec((1, D_MODEL, D_MODEL), lambda l, n: (l, 0, n)),
            pl.BlockSpec((1, 1, D_MODEL), lambda l, n: (l, 0, n)),
        ],
        out_specs=pl.BlockSpec((1, rows, D_MODEL), lambda l, n: (l, 0, n)),
        out_shape=jax.ShapeDtypeStruct((depth, rows, 3 * D_MODEL), F32),
        compiler_params=_params(2),
        name="ada_mod",
    )(s, w_ada, b_ada.reshape(depth, 1, 3 * D_MODEL))


def _rope(t, cos, sin_signed, lane_lo):
    partner = jnp.where(lane_lo, pltpu.roll(t, LANES - 16, axis=1), pltpu.roll(t, 16, axis=1))
    return t * cos + partner * sin_signed


def _ab_in_kernel(xc_ref, xl_ref, xpc_ref, xpl_ref, xnc_ref, xnl_ref, mod_ref, w_ref, cw_ref, cos_ref, sin_ref,
                  q_ref, k_ref, v_ref, sga_ref, obg_ref, *, n_tiles, ctx_tiles):
    i = pl.program_id(1)
    tm = xc_ref.shape[1]
    is_ctx = i < ctx_tiles
    h = _modulate(jnp.where(is_ctx, xc_ref[0], xl_ref[0]), mod_ref).astype(BF16)

    def proj(lhs, lo, hi):
        return _dot(lhs, w_ref[:, lo:hi])

    o_q, o_k, o_v, o_ga = 0, A_WIDTH, A_WIDTH + A_KV_WIDTH, A_WIDTH + 2 * A_KV_WIDTH
    o_xb = o_ga + A_WIDTH
    o_bg, o_cg, o_gb = o_xb + B_WIDTH, o_xb + 2 * B_WIDTH, o_xb + 3 * B_WIDTH

    cos = cos_ref[...]
    sin_signed = sin_ref[...]
    lane = lax.broadcasted_iota(jnp.int32, (tm, LANES), 1)
    lane_lo = (lane % 32) < 16
    q = proj(h, o_q, o_k)
    for j in range(A_WIDTH // LANES):
        qj = _rope(q[:, j * LANES:(j + 1) * LANES], cos, sin_signed, lane_lo)
        q_ref[0, :, j * LANES:(j + 1) * LANES] = qj * (A_HEAD_DIM ** -0.5 * LOG2_E)
    k_ref[0] = _rope(proj(h, o_k, o_v), cos, sin_signed, lane_lo)
    v_ref[0] = proj(h, o_v, o_ga)
    sga_ref[0] = _silu(proj(h, o_ga, o_xb))

    u = proj(h, o_cg, o_gb) * proj(h, o_xb, o_bg)
    halo = jnp.concatenate([jnp.where(is_ctx, xpc_ref[0], xpl_ref[0]), jnp.where(is_ctx, xnc_ref[0], xnl_ref[0])],
                           axis=0)
    hh = _modulate(halo, mod_ref).astype(BF16)
    uh = proj(hh, o_cg, o_gb) * proj(hh, o_xb, o_bg)
    left_ok = jnp.logical_and(i != 0, i != ctx_tiles)
    right_ok = jnp.logical_and(i != ctx_tiles - 1, i != n_tiles - 1)
    u_left = jnp.where(left_ok, uh[SUBLANES - 1:SUBLANES], 0.0)
    u_right = jnp.where(right_ok, uh[SUBLANES:SUBLANES + 1], 0.0)
    row = lax.broadcasted_iota(jnp.int32, u.shape, 0)
    u_prev = jnp.where(row == 0, u_left, pltpu.roll(u, 1, axis=0))
    u_next = jnp.where(row == tm - 1, u_right, pltpu.roll(u, tm - 1, axis=0))
    conv = u_prev * cw_ref[0:1, :] + u * cw_ref[1:2, :] + u_next * cw_ref[2:3, :]
    obg_ref[0] = proj(h, o_bg, o_cg) * conv * _silu(proj(h, o_gb, o_gb + B_WIDTH))


def _stream_len(stream):
    return stream[1].shape[1] + stream[2]


def _stream_windows(stream, rows, ctx_len, index):
    ctx_arr, lat_arr, shift = stream
    ctx_blocks, n_blocks, skip = ctx_len // rows, _stream_len(stream) // rows, shift // rows
    specs = [
        pl.BlockSpec((1, rows, D_MODEL), lambda b, i: (b, jnp.clip(index(i), 0, ctx_blocks - 1), 0)),
        pl.BlockSpec((1, rows, D_MODEL), lambda b, i: (b, jnp.clip(index(i), ctx_blocks, n_blocks - 1) - skip, 0)),
    ]
    return specs, [ctx_arr, lat_arr]


def _ab_in(stream, modsel, w_in, conv_w, cos_t, sin_t, ctx_len):
    B, L = stream[0].shape[0], _stream_len(stream)
    tm = ROW_TILE
    n_tiles = L // tm
    ctx_tiles = ctx_len // tm
    halo_blocks = tm // SUBLANES
    kern = functools.partial(_ab_in_kernel, n_tiles=n_tiles, ctx_tiles=ctx_tiles)
    width = w_in.shape[1]
    tok = lambda w: pl.BlockSpec((1, tm, w), lambda b, i: (b, i, 0))
    x_specs, x_args = [], []
    for rows, index in ((tm, lambda i: i), (SUBLANES, lambda i: i * halo_blocks - 1),
                        (SUBLANES, lambda i: (i + 1) * halo_blocks)):
        specs, args = _stream_windows(stream, rows, ctx_len, index)
        x_specs += specs
        x_args += args
    return pl.pallas_call(
        kern,
        grid=(B, n_tiles),
        in_specs=x_specs + [
            pl.BlockSpec((1, 1, 3, D_MODEL), lambda b, i: (b, (i >= ctx_tiles).astype(jnp.int32), 0, 0)),
            pl.BlockSpec((D_MODEL, width), lambda b, i: (0, 0)),
            pl.BlockSpec((3, B_WIDTH), lambda b, i: (0, 0)),
            pl.BlockSpec((tm, LANES), lambda b, i: (i, 0)),
            pl.BlockSpec((tm, LANES), lambda b, i: (i, 0)),
        ],
        out_specs=[tok(A_WIDTH), tok(A_KV_WIDTH), tok(A_KV_WIDTH), tok(A_WIDTH), tok(B_WIDTH)],
        out_shape=[
            jax.ShapeDtypeStruct((B, L, A_WIDTH), F32),
            jax.ShapeDtypeStruct((B, L, A_KV_WIDTH), F32),
            jax.ShapeDtypeStruct((B, L, A_KV_WIDTH), F32),
            jax.ShapeDtypeStruct((B, L, A_WIDTH), F32),
            jax.ShapeDtypeStruct((B, L, B_WIDTH), F32),
        ],
        compiler_params=_params(2),
        name="ab_in",
    )(*x_args, modsel, w_in, conv_w, cos_t, sin_t)


def _ab_attn_kernel(q_ref, kx_ref, kp_ref, kc_ref, kn_ref, vx_ref, vp_ref, vc_ref, vn_ref,
                    sga_ref, obg_ref, xc_ref, xl_ref, mod_ref, sink_ref, w_ref, g_ref, b_ref, o_ref,
                    *, n_blk, ctx_blk):
    i = pl.program_id(1)
    W = WINDOW
    n_sub = q_ref.shape[1] // W
    lc = kx_ref.shape[1]
    half = LANES // 2
    group = A_HEADS // A_KV_HEADS

    k_ctx, v_ctx = kx_ref[0].astype(BF16), vx_ref[0].astype(BF16)
    k_loc = jnp.concatenate([kp_ref[0], kc_ref[0], kn_ref[0]], axis=0).astype(BF16)
    v_loc = jnp.concatenate([vp_ref[0], vc_ref[0], vn_ref[0]], axis=0).astype(BF16)

    is_lat = i * n_sub >= ctx_blk
    ri = lax.broadcasted_iota(jnp.int32, (W, W), 0)
    ci = lax.broadcasted_iota(jnp.int32, (W, W), 1)
    neg = jnp.float32(-jnp.inf)
    lane = lax.broadcasted_iota(jnp.int32, (W, LANES), 1)
    low = lane < half

    def logits(sb, kvh):
        blk = i * n_sub + sb
        prev_ok = jnp.logical_and(is_lat, blk - 1 >= ctx_blk)
        next_ok = jnp.logical_and(is_lat, blk + 1 <= n_blk - 1)
        bias = jnp.concatenate([
            jnp.where(jnp.logical_and(prev_ok, ci >= ri), 0.0, neg),
            jnp.where(jnp.logical_and(is_lat, ci >= 0), 0.0, neg),
            jnp.where(jnp.logical_and(next_ok, ci <= ri), 0.0, neg),
        ], axis=1)
        bias = jnp.concatenate([bias] * group, axis=0)
        on_half = low if kvh == 0 else jnp.logical_not(low)
        stack, sinks = [], []
        for g in range(group):
            hd = kvh * group + g
            qh = q_ref[0, sb * W:(sb + 1) * W, (hd // 2) * LANES:(hd // 2 + 1) * LANES]
            if hd % 2 != kvh:
                qh = pltpu.roll(qh, half, axis=1)
            stack.append(jnp.where(on_half, qh, 0.0))
            sinks.append(jnp.broadcast_to(sink_ref[hd:hd + 1, :] * LOG2_E, (W, LANES)))
        k_sb = jnp.concatenate([k_ctx, k_loc[sb * W:(sb + 3) * W]], axis=0)
        s = _dot_nt(jnp.concatenate(stack, axis=0).astype(BF16), k_sb)
        return jnp.concatenate([s[:, :lc], s[:, lc:] + bias], axis=1), jnp.concatenate(sinks, axis=0)

    def attend(sb, s, sk):
        v_sb = jnp.concatenate([v_ctx, v_loc[sb * W:(sb + 3) * W]], axis=0)
        m = jnp.maximum(jnp.max(s, axis=-1, keepdims=True), sk)
        e = jnp.exp2(s - jnp.concatenate([m] * (s.shape[1] // LANES), axis=1))
        denom = jnp.sum(e, axis=-1, keepdims=True) + jnp.exp2(sk - m)
        return _dot(e.astype(BF16), v_sb) / denom

    stacks = [(sb, kvh) for sb in range(n_sub) for kvh in range(A_KV_HEADS)]
    pieces = {}
    nxt = logits(*stacks[0])
    for idx, (sb, kvh) in enumerate(stacks):
        cur = nxt
        if idx + 1 < len(stacks):
            nxt = logits(*stacks[idx + 1])
        pv = attend(sb, *cur)
        for g in range(group):
            hd = kvh * group + g
            piece = pv[g * W:(g + 1) * W]
            pieces[sb, hd] = piece if hd % 2 == kvh else pltpu.roll(piece, half, axis=1)
    o_a = jnp.concatenate([
        jnp.concatenate([jnp.where(low, pieces[sb, 2 * j], pieces[sb, 2 * j + 1]) for j in range(A_HEADS // 2)], axis=1)
        for sb in range(n_sub)], axis=0)

    y_in = jnp.concatenate([o_a * sga_ref[0], obg_ref[0]], axis=1).astype(BF16)
    y = _dot(y_in, w_ref[...])
    gate = mod_ref[0, 0, 2:3, :]
    x = jnp.where(is_lat, xl_ref[0], xc_ref[0])
    o_ref[0] = _layer_norm(DEEPNORM_ALPHA * x + gate * y, g_ref[...], b_ref[...])


def _ab_attn(stream, modsel, q, k, v, sga, obg, sink_b, w_out, ln_g, ln_b, ctx_len):
    B, L = stream[0].shape[0], _stream_len(stream)
    T = ROW_TILE
    W = WINDOW
    n_sub = T // W
    n_blk = L // W
    ctx_blk = ctx_len // W
    kern = functools.partial(_ab_attn_kernel, n_blk=n_blk, ctx_blk=ctx_blk)
    tok = lambda w: pl.BlockSpec((1, T, w), lambda b, i: (b, i, 0))
    prev = pl.BlockSpec((1, W, A_KV_WIDTH), lambda b, i: (b, jnp.maximum(i * n_sub - 1, 0), 0))
    nxt = pl.BlockSpec((1, W, A_KV_WIDTH), lambda b, i: (b, jnp.minimum((i + 1) * n_sub, n_blk - 1), 0))
    ctxs = pl.BlockSpec((1, ctx_len, A_KV_WIDTH), lambda b, i: (b, 0, 0))
    row = pl.BlockSpec((1, D_MODEL), lambda b, i: (0, 0))
    x_specs, x_args = _stream_windows(stream, T, ctx_len, lambda i: i)
    return pl.pallas_call(
        kern,
        grid=(B, L // T),
        in_specs=[
            tok(A_WIDTH),
            ctxs, prev, tok(A_KV_WIDTH), nxt,
            ctxs, prev, tok(A_KV_WIDTH), nxt,
            tok(A_WIDTH), tok(B_WIDTH), *x_specs,
            pl.BlockSpec((1, 1, 3, D_MODEL), lambda b, i: (b, (i * n_sub >= ctx_blk).astype(jnp.int32), 0, 0)),
            pl.BlockSpec((A_HEADS, LANES), lambda b, i: (0, 0)),
            pl.BlockSpec((D_MODEL, D_MODEL), lambda b, i: (0, 0)),
            row, row,
        ],
        out_specs=tok(D_MODEL),
        out_shape=jax.ShapeDtypeStruct((B, L, D_MODEL), F32),
        compiler_params=_params(2),
        name="ab_attn_out",
    )(q, k, k, k, k, v, v, v, v, sga, obg, *x_args, modsel, sink_b, w_out, ln_g, ln_b)


def _forget_gate(z, lb):
    e = jnp.exp(-jnp.abs(z))
    return jnp.where(z >= 0, 1.0 + lb * e, lb + e) / (1.0 + e)


def _c_in_kernel(x_ref, mod_ref, w_ref, lb_ref, q_ref, ff_ref, fb_ref, v_ref, sg_ref):
    h = _modulate(x_ref[0], mod_ref).astype(BF16)

    def proj(idx):
        return _dot(h, w_ref[:, idx * C_WIDTH:(idx + 1) * C_WIDTH])

    def put(ref, val):
        tm = val.shape[0]
        for hd in range(C_HEADS):
            g, hh = divmod(hd, HEAD_GROUP)
            ref[0, g, pl.ds(hh, tm, stride=HEAD_GROUP), :] = val[:, hd * LANES:(hd + 1) * LANES]

    put(q_ref, _silu(proj(0)) * (C_KEY_DIM ** -0.5))
    put(v_ref, proj(3))
    sg_ref[0] = _silu(proj(4)).astype(sg_ref.dtype)
    for d, f_ref in enumerate((ff_ref, fb_ref)):
        put(f_ref, _forget_gate(proj(1 + d), jnp.clip(lb_ref[d:d + 1, :], 0.0, 1.0)))


def _c_in(xcat, modsel, w_in, lb, ctx_len):
    B, L, _ = xcat.shape
    tm = ROW_TILE
    n_tiles = L // tm
    ctx_tiles = ctx_len // tm
    width = w_in.shape[1]
    groups = C_HEADS // HEAD_GROUP
    hm = pl.BlockSpec((1, groups, tm * HEAD_GROUP, LANES), lambda b, i: (b, 0, i, 0))
    hm_shape = jax.ShapeDtypeStruct((B, groups, L * HEAD_GROUP, LANES), F32)
    return pl.pallas_call(
        _c_in_kernel,
        grid=(B, n_tiles),
        in_specs=[
            pl.BlockSpec((1, tm, D_MODEL), lambda b, i: (b, i, 0)),
            pl.BlockSpec((1, 1, 3, D_MODEL), lambda b, i: (b, (i >= ctx_tiles).astype(jnp.int32), 0, 0)),
            pl.BlockSpec((D_MODEL, width), lambda b, i: (0, 0)),
            pl.BlockSpec((2, C_WIDTH), lambda b, i: (0, 0)),
        ],
        out_specs=[hm, hm, hm, hm, pl.BlockSpec((1, tm, C_WIDTH), lambda b, i: (b, i, 0))],
        out_shape=[hm_shape] * 4 + [jax.ShapeDtypeStruct((B, L, C_WIDTH), BF16)],
        compiler_params=_params(2),
        name="c_in",
    )(xcat, modsel, w_in, lb)


def _query_blocks(n, m, reverse):
    return [p for p in range(0, n, m) if ((p // m) % 2 == 1) != reverse]


N_SCAN_OPERANDS = len(SCAN_LEVELS) + 3


SCAN_CHUNK = 16
LOOKAHEAD = 2


def _scan_elementwise(q_ref, f_ref, v_ref, reverse, out_ref, decay_ref):
    assert SUBLANES == 2 * HEAD_GROUP
    n_rows, width = q_ref.shape
    chunk_rows = SCAN_CHUNK * HEAD_GROUP
    grouped = lambda t: t.reshape(t.shape[0] // SUBLANES, SUBLANES, width)
    low1 = lax.broadcasted_iota(jnp.int32, (1, SUBLANES, width), 1) < HEAD_GROUP
    swap = lambda t: pltpu.roll(t, HEAD_GROUP, axis=1)

    def block_total(t):
        return jnp.where(low1, t, swap(t)) if reverse else jnp.where(low1, swap(t), t)

    def double(q, k, run, rest, g):
        z, new_run, new_rest = [], [], []
        for a in range(0, run.shape[0], 2 * g):
            lo, hi = slice(a, a + g), slice(a + g, a + 2 * g)
            first, last = (hi, lo) if reverse else (lo, hi)
            edge = (lambda blk: run[blk.start:blk.start + 1]) if reverse else (lambda blk: run[blk.stop - 1:blk.stop])
            parts = {
                first: (k[first] * rest[first], run[first], rest[first] * block_total(edge(last))),
                last: (q[last] * run[last], run[last] * block_total(edge(first)), rest[last]),
            }
            for blk in (lo, hi):
                z.append(parts[blk][0])
                new_run.append(parts[blk][1])
                new_rest.append(parts[blk][2])
        return tuple(jnp.concatenate(t, axis=0) for t in (z, new_run, new_rest))

    def put(idx, rows, t):
        out_ref[idx, rows, :] = t.reshape(t.shape[0] * SUBLANES, width)

    low_levels = [m for m in SCAN_LEVELS if m < SCAN_CHUNK]
    n_low, n_lvl = len(low_levels), len(SCAN_LEVELS)
    chunks = [slice(r0, r0 + chunk_rows) for r0 in range(0, n_rows, chunk_rows)]
    q_run, k_rest, total = [], [], []
    for rows in chunks:
        q, f, v = grouped(q_ref[rows, :]), grouped(f_ref[rows, :]), grouped(v_ref[rows, :])
        k = 1.0 - f
        put(n_lvl + 2, rows, jnp.sum(q * k, axis=-1, keepdims=True) * v)
        second = low1 if reverse else jnp.logical_not(low1)
        other = swap(f)
        put(0, rows, jnp.where(second, q * f, k))
        run = jnp.where(second, f * other, f)
        rest = jnp.where(second, 1.0, other)
        for lvl, m in enumerate(low_levels[1:], start=1):
            z, run, rest = double(q, k, run, rest, m // 2)
            put(lvl, rows, z)
        q_run.append(q * run)
        k_rest.append(k * rest)
        total.append(block_total(run[0:1] if reverse else run[run.shape[0] - 1:]))

    mul = lambda a, b: b if a is None else a * b
    before, after = [None] * len(chunks), [None] * len(chunks)
    for lvl, m in enumerate(SCAN_LEVELS[n_low:], start=n_low):
        g = m // SCAN_CHUNK
        new_total = []
        for a in range(0, len(chunks), 2 * g):
            lo, hi = list(range(a, a + g)), list(range(a + g, a + 2 * g))
            first, last = (hi, lo) if reverse else (lo, hi)
            t_first, t_last = total[first[0] // g], total[last[0] // g]
            for c in first:
                put(lvl, chunks[c], mul(after[c], k_rest[c]))
                after[c] = mul(after[c], t_last)
            for c in last:
                put(lvl, chunks[c], mul(before[c], q_run[c]))
                before[c] = mul(before[c], t_first)
            new_total.append(t_first * t_last)
        total = new_total
    for c, rows in enumerate(chunks):
        put(n_lvl, rows, mul(before[c], q_run[c]))
        put(n_lvl + 1, rows, mul(after[c], k_rest[c]))
    decay_ref[...] = total[0][0]


def _scan_products(load, v, decay, state_t, mask_ref, d):
    n = v.shape[0]
    reverse = d == 1
    n_lvl = len(SCAN_LEVELS)
    blocks = [None] * (n // SUBLANES)

    def add_rows(start, term):
        for r in range(0, term.shape[0], SUBLANES):
            b = (start + r) // SUBLANES
            piece = term[r:r + SUBLANES]
            blocks[b] = piece if blocks[b] is None else blocks[b] + piece

    for lvl, m in enumerate(SCAN_LEVELS):
        z = load(lvl)
        if m < SUBLANES:
            zb = z.astype(BF16)
            add_rows(0, _dot_nt(zb, zb) * mask_ref[d, lvl])
        else:
            starts = _query_blocks(n, m, reverse)
            zq = jnp.concatenate([z[p:p + m] for p in starts], axis=0).astype(BF16)
            mask = jnp.concatenate([mask_ref[d, lvl, p:p + m, :] for p in starts], axis=0)
            term = _dot_nt(zq, z.astype(BF16)) * mask
            for idx, p in enumerate(starts):
                add_rows(p, term[idx * m:(idx + 1) * m])
    weights = jnp.concatenate(blocks, axis=0).astype(BF16)
    q_in, k_out, own = load(n_lvl).astype(BF16), load(n_lvl + 1).astype(BF16), load(n_lvl + 2)
    o_partial = _dot_nt(q_in, state_t.astype(BF16)) + own
    new_state = state_t * decay + _dot(v.T.astype(BF16), k_out)
    return weights, v.astype(BF16), o_partial, new_state


def _c_scan_kernel(qf_ref, ff_ref, vf_ref, qb_ref, fb_ref, vb_ref, mask_ref, of_ref, ob_ref,
                   st_ref, op0_ref, op1_ref, decay0_ref, decay1_ref):
    @pl.when(pl.program_id(1) == 0)
    def _():
        st_ref[...] = jnp.zeros(st_ref.shape, F32)

    T = SCAN_TILE
    dirs = ((qf_ref, ff_ref, vf_ref, of_ref), (qb_ref, fb_ref, vb_ref, ob_ref))
    units = [(g, d) for g in range(C_HEADS // HEAD_GROUP) for d in range(2)]
    op_refs, decay_refs = (op0_ref, op1_ref), (decay0_ref, decay1_ref)

    def elementwise(idx):
        g, d = units[idx]
        q_r, f_r, v_r, _ = dirs[d]
        slot = idx % 2
        _scan_elementwise(q_r.at[0, g], f_r.at[0, g], v_r.at[0, g], d == 1, op_refs[slot], decay_refs[slot])

    def finish(job):
        (d, hd), (weights, v_bf, o_partial, _) = job
        o_r = dirs[d][3]
        o_r[0, hd] = (o_partial + _dot(weights, v_bf)).astype(o_r.dtype)

    elementwise(0)
    pending = []
    for idx, (g, d) in enumerate(units):
        if idx + 1 < len(units):
            elementwise(idx + 1)
        slot = idx % 2
        for hh in range(HEAD_GROUP):
            hd = g * HEAD_GROUP + hh
            rows = pl.ds(hh, T, stride=HEAD_GROUP)
            load = lambda i, rows=rows, slot=slot: op_refs[slot][i, rows, :]
            job = ((d, hd), _scan_products(load, dirs[d][2][0, g, rows, :], decay_refs[slot][hh:hh + 1, :],
                                           st_ref[d, hd], mask_ref, d))
            st_ref[d, hd] = job[1][3]
            pending.append(job)
            if len(pending) > LOOKAHEAD:
                finish(pending.pop(0))
    for job in pending:
        finish(job)


def _scan_masks():
    n = SCAN_TILE
    c = np.arange(n)[:, None]
    s = np.arange(n)[None, :]
    fwd = []
    for m in SCAN_LEVELS:
        fwd.append(((c // (2 * m) == s // (2 * m)) & (c % (2 * m) >= m) & (s % (2 * m) < m)).astype(np.float32))
    fwd = np.stack(fwd)
    return jnp.asarray(np.stack([fwd, fwd.transpose(0, 2, 1)]))


def _c_scan(q, ff, fb, v, ctx_len):
    B, groups, rows, _ = q.shape
    H = groups * HEAD_GROUP
    L = rows // HEAD_GROUP
    T = SCAN_TILE
    n = L // T
    nc = ctx_len // T
    masks = _scan_masks()
    back = lambda j: jnp.where(j < nc, nc - 1 - j, n - 1 - (j - nc))
    fwd_in = pl.BlockSpec((1, groups, T * HEAD_GROUP, LANES), lambda b, j: (b, 0, j, 0))
    bwd_in = pl.BlockSpec((1, groups, T * HEAD_GROUP, LANES), lambda b, j: (b, 0, back(j), 0))
    fwd = pl.BlockSpec((1, H, T, LANES), lambda b, j: (b, 0, j, 0))
    bwd = pl.BlockSpec((1, H, T, LANES), lambda b, j: (b, 0, back(j), 0))
    hm_shape = jax.ShapeDtypeStruct((B, H, L, LANES), BF16)
    return pl.pallas_call(
        _c_scan_kernel,
        grid=(B, n),
        in_specs=[fwd_in, fwd_in, fwd_in, bwd_in, bwd_in, bwd_in,
                  pl.BlockSpec(masks.shape, lambda b, j: (0, 0, 0, 0))],
        out_specs=[fwd, bwd],
        out_shape=[hm_shape, hm_shape],
        scratch_shapes=[pltpu.VMEM((2, H, C_VAL_DIM, C_KEY_DIM), F32),
                        pltpu.VMEM((N_SCAN_OPERANDS, T * HEAD_GROUP, LANES), F32),
                        pltpu.VMEM((N_SCAN_OPERANDS, T * HEAD_GROUP, LANES), F32),
                        pltpu.VMEM((SUBLANES, LANES), F32),
                        pltpu.VMEM((SUBLANES, LANES), F32)],
        compiler_params=_params(2),
        name="c_scan",
    )(q, ff, v, q, fb, v, masks)


def _c_out_kernel(of_ref, ob_ref, sg_ref, x_ref, mod_ref, gn_ref, w_ref, g_ref, b_ref, o_ref):
    cols = []
    for hd in range(C_HEADS):
        o = of_ref[0, hd].astype(F32) + ob_ref[0, hd].astype(F32)
        ms = jnp.mean(o * o, axis=-1, keepdims=True)
        cols.append(o * lax.rsqrt(ms + RMS_EPS) * gn_ref[...])
    y_in = (jnp.concatenate(cols, axis=1) * sg_ref[0]).astype(BF16)
    y = _dot(y_in, w_ref[...])
    gate = mod_ref[0, 0, 2:3, :]
    o_ref[0] = _layer_norm(DEEPNORM_ALPHA * x_ref[0] + gate * y, g_ref[...], b_ref[...])


def _c_out(xcat, modsel, o_f, o_b, sg, gnorm, w_out, ln_g, ln_b, ctx_len, latent_only):
    B, L, _ = xcat.shape
    tm = ROW_TILE
    ctx_tiles = ctx_len // tm
    skip = ctx_tiles if latent_only else 0
    n_tiles = L // tm - skip
    hm = pl.BlockSpec((1, C_HEADS, tm, LANES), lambda b, i: (b, 0, i + skip, 0))
    tok = pl.BlockSpec((1, tm, D_MODEL), lambda b, i: (b, i + skip, 0))
    row = pl.BlockSpec((1, D_MODEL), lambda b, i: (0, 0))
    return pl.pallas_call(
        _c_out_kernel,
        grid=(B, n_tiles),
        in_specs=[
            hm, hm, tok, tok,
            pl.BlockSpec((1, 1, 3, D_MODEL), lambda b, i: (b, (i + skip >= ctx_tiles).astype(jnp.int32), 0, 0)),
            pl.BlockSpec((1, LANES), lambda b, i: (0, 0)),
            pl.BlockSpec((D_MODEL, D_MODEL), lambda b, i: (0, 0)),
            row, row,
        ],
        out_specs=pl.BlockSpec((1, tm, D_MODEL), lambda b, i: (b, i, 0)),
        out_shape=jax.ShapeDtypeStruct((B, n_tiles * tm, D_MODEL), F32),
        compiler_params=_params(2),
        name="c_out",
    )(o_f, o_b, sg, xcat, modsel, gnorm, w_out, ln_g, ln_b)


def _rope_tables(seq, ctx_len):
    t = jnp.arange(seq)
    freqs = ROPE_BASE ** (-jnp.arange(16, dtype=F32) / 16)
    ang_r = (t // GRID_W).astype(F32)[:, None] * freqs[None, :]
    ang_c = (t % GRID_W).astype(F32)[:, None] * freqs[None, :]
    cos = jnp.concatenate([jnp.cos(ang_r)] * 2 + [jnp.cos(ang_c)] * 2, axis=-1)
    sin = jnp.concatenate([-jnp.sin(ang_r), jnp.sin(ang_r), -jnp.sin(ang_c), jnp.sin(ang_c)], axis=-1)
    cos = jnp.concatenate([jnp.ones((ctx_len, 64), F32), cos], axis=0)
    sin = jnp.concatenate([jnp.zeros((ctx_len, 64), F32), sin], axis=0)
    return jnp.tile(cos, (1, 2)), jnp.tile(sin, (1, 2))


def kernel(x, c, ctx, c_ctx, w_ada, b_ada, ln_g, ln_b, w_in_ab, w_out_ab, sink_ab, conv_ab,
           w_in_c, w_out_c, lb_c, gnorm_c):
    B, seq, _ = x.shape
    ctx_len = ctx.shape[1]
    depth = w_ada.shape[0]
    assert seq % ROW_TILE == 0 and ctx_len % ROW_TILE == 0 and B <= 8

    s = jnp.concatenate([c, c_ctx[None, :], jnp.zeros((16 - B - 1, D_MODEL), F32)], axis=0)
    mod_all = _ada(s, w_ada, b_ada)
    mod_lat = mod_all[:, :B].reshape(depth, B, 1, 3, D_MODEL)
    mod_ctx = jnp.broadcast_to(mod_all[:, B].reshape(depth, 1, 1, 3, D_MODEL), mod_lat.shape)
    modsel = jnp.concatenate([mod_ctx, mod_lat], axis=2)

    lb_p = jax.nn.softmax(lb_c.astype(F32), axis=1)
    lb_all = jnp.cumsum(lb_p, axis=1) - lb_p[:, :1]

    cos_t, sin_t = _rope_tables(seq, ctx_len)
    xcat = None
    for l in range(depth):
        j = l // 2
        g_row, b_row = ln_g[l][None, :], ln_b[l][None, :]
        if l % 2 == 0:
            stream = (ctx, x, ctx_len) if xcat is None else (xcat, xcat, 0)
            q, k, v, sga, obg = _ab_in(stream, modsel[l], w_in_ab[j].astype(BF16), conv_ab[j], cos_t, sin_t, ctx_len)
            sink_b = jnp.broadcast_to(sink_ab[j].astype(F32)[:, None], (A_HEADS, LANES))
            xcat = _ab_attn(stream, modsel[l], q, k, v, sga, obg, sink_b, w_out_ab[j].astype(BF16),
                            g_row, b_row, ctx_len)
        else:
            q, ff, fb, v, sg = _c_in(xcat, modsel[l], w_in_c[j].astype(BF16), lb_all[:, j], ctx_len)
            o_f, o_b = _c_scan(q, ff, fb, v, ctx_len)
            xcat = _c_out(xcat, modsel[l], o_f, o_b, sg, gnorm_c[j][None, :], w_out_c[j].astype(BF16),
                          g_row, b_row, ctx_len, latent_only=(l == depth - 1))
    return xcat if depth % 2 == 0 else xcat[:, ctx_len:]
```

```python
import functools

import numpy as np
import jax
import jax.numpy as jnp
from jax import lax
from jax.experimental import pallas as pl
from jax.experimental.pallas import tpu as pltpu

F32 = jnp.float32
BF16 = jnp.bfloat16

D_MODEL = 1024
DEPTH = 4
GRID_W = 64
A_HEADS = 8
A_KV_HEADS = 2
A_HEAD_DIM = 64
A_WIDTH = A_HEADS * A_HEAD_DIM
A_KV_WIDTH = A_KV_HEADS * A_HEAD_DIM
WINDOW = 128
ROPE_BASE = 10000.0
B_WIDTH = D_MODEL // 2
C_HEADS = 8
C_KEY_DIM = 128
C_VAL_DIM = D_MODEL // C_HEADS
C_WIDTH = C_HEADS * C_KEY_DIM
DEEPNORM_ALPHA = (2 * DEPTH) ** 0.25
LN_EPS = 1e-5
RMS_EPS = 1e-6
LOG2_E = 1.4426950408889634

LANES = 128
SUBLANES = 8
ROW_TILE = 256
SCAN_TILE = 128
SCAN_LEVELS = (1, 2, 4, 8, 16, 32, 64)
HEAD_GROUP = 4
VMEM_LIMIT = 56 * 1024 * 1024


def _params(n_axes):
    return pltpu.CompilerParams(dimension_semantics=("arbitrary",) * n_axes, vmem_limit_bytes=VMEM_LIMIT)


def _dot(a, b):
    return jnp.dot(a, b, preferred_element_type=F32)


def _dot_nt(a, b):
    return lax.dot_general(a, b, (((1,), (1,)), ((), ())), preferred_element_type=F32)


def _silu(t):
    return t * jax.nn.sigmoid(t)


def _modulate(x, mod_ref):
    shift = mod_ref[0, 0, 0:1, :]
    scale = mod_ref[0, 0, 1:2, :]
    return x * (1 + scale) + shift


def _layer_norm(r, g, b):
    mu = jnp.mean(r, axis=-1, keepdims=True)
    rc = r - mu
    var = jnp.mean(rc * rc, axis=-1, keepdims=True)
    return rc * lax.rsqrt(var + LN_EPS) * g + b


def _ada_kernel(s_ref, w_ref, b_ref, o_ref):
    s = _silu(s_ref[...]).astype(BF16)
    o_ref[0] = _dot(s, w_ref[0].astype(BF16)) + b_ref[0]


def _ada(s, w_ada, b_ada):
    depth = w_ada.shape[0]
    rows = s.shape[0]
    n_col = 3 * D_MODEL // D_MODEL
    return pl.pallas_call(
        _ada_kernel,
        grid=(depth, n_col),
        in_specs=[
            pl.BlockSpec((rows, D_MODEL), lambda l, n: (0, 0)),
            pl.BlockSpec((1, D_MODEL, D_MODEL), lambda l, n: (l, 0, n)),
            pl.BlockSpec((1, 1, D_MODEL), lambda l, n: (l, 0, n)),
        ],
        out_specs=pl.BlockSpec((1, rows, D_MODEL), lambda l, n: (l, 0, n)),
        out_shape=jax.ShapeDtypeStruct((depth, rows, 3 * D_MODEL), F32),
        compiler_params=_params(2),
        name="ada_mod",
    )(s, w_ada, b_ada.reshape(depth, 1, 3 * D_MODEL))


def _rope(t, cos, sin_signed, lane_lo):
    partner = jnp.where(lane_lo, pltpu.roll(t, LANES - 16, axis=1), pltpu.roll(t, 16, axis=1))
    return t * cos + partner * sin_signed


def _ab_in_kernel(xc_ref, xl_ref, xpc_ref, xpl_ref, xnc_ref, xnl_ref, mod_ref, w_ref, cw_ref, cos_ref, sin_ref,
                  q_ref, k_ref, v_ref, sga_ref, obg_ref, *, n_tiles, ctx_tiles):
    i = pl.program_id(1)
    tm = xc_ref.shape[1]
    is_ctx = i < ctx_tiles
    h = _modulate(jnp.where(is_ctx, xc_ref[0], xl_ref[0]), mod_ref).astype(BF16)

    def proj(lhs, lo, hi):
        return _dot(lhs, w_ref[:, lo:hi])

    o_q, o_k, o_v, o_ga = 0, A_WIDTH, A_WIDTH + A_KV_WIDTH, A_WIDTH + 2 * A_KV_WIDTH
    o_xb = o_ga + A_WIDTH
    o_bg, o_cg, o_gb = o_xb + B_WIDTH, o_xb + 2 * B_WIDTH, o_xb + 3 * B_WIDTH

    cos = cos_ref[...]
    sin_signed = sin_ref[...]
    lane = lax.broadcasted_iota(jnp.int32, (tm, LANES), 1)
    lane_lo = (lane % 32) < 16
    q = proj(h, o_q, o_k)
    for j in range(A_WIDTH // LANES):
        qj = _rope(q[:, j * LANES:(j + 1) * LANES], cos, sin_signed, lane_lo)
        q_ref[0, :, j * LANES:(j + 1) * LANES] = qj * (A_HEAD_DIM ** -0.5 * LOG2_E)
    k_ref[0] = _rope(proj(h, o_k, o_v), cos, sin_signed, lane_lo)
    v_ref[0] = proj(h, o_v, o_ga)
    sga_ref[0] = _silu(proj(h, o_ga, o_xb))

    u = proj(h, o_cg, o_gb) * proj(h, o_xb, o_bg)
    halo = jnp.concatenate([jnp.where(is_ctx, xpc_ref[0], xpl_ref[0]), jnp.where(is_ctx, xnc_ref[0], xnl_ref[0])],
                           axis=0)
    hh = _modulate(halo, mod_ref).astype(BF16)
    uh = proj(hh, o_cg, o_gb) * proj(hh, o_xb, o_bg)
    left_ok = jnp.logical_and(i != 0, i != ctx_tiles)
    right_ok = jnp.logical_and(i != ctx_tiles - 1, i != n_tiles - 1)
    u_left = jnp.where(left_ok, uh[SUBLANES - 1:SUBLANES], 0.0)
    u_right = jnp.where(right_ok, uh[SUBLANES:SUBLANES + 1], 0.0)
    row = lax.broadcasted_iota(jnp.int32, u.shape, 0)
    u_prev = jnp.where(row == 0, u_left, pltpu.roll(u, 1, axis=0))
    u_next = jnp.where(row == tm - 1, u_right, pltpu.roll(u, tm - 1, axis=0))
    conv = u_prev * cw_ref[0:1, :] + u * cw_ref[1:2, :] + u_next * cw_ref[2:3, :]
    obg_ref[0] = proj(h, o_bg, o_cg) * conv * _silu(proj(h, o_gb, o_gb + B_WIDTH))


def _stream_len(stream):
    return stream[1].shape[1] + stream[2]


def _stream_windows(stream, rows, ctx_len, index):
    ctx_arr, lat_arr, shift = stream
    ctx_blocks, n_blocks, skip = ctx_len // rows, _stream_len(stream) // rows, shift // rows
    specs = [
        pl.BlockSpec((1, rows, D_MODEL), lambda b, i: (b, jnp.clip(index(i), 0, ctx_blocks - 1), 0)),
        pl.BlockSpec((1, rows, D_MODEL), lambda b, i: (b, jnp.clip(index(i), ctx_blocks, n_blocks - 1) - skip, 0)),
    ]
    return specs, [ctx_arr, lat_arr]


def _ab_in(stream, modsel, w_in, conv_w, cos_t, sin_t, ctx_len):
    B, L = stream[0].shape[0], _stream_len(stream)
    tm = ROW_TILE
    n_tiles = L // tm
    ctx_tiles = ctx_len // tm
    halo_blocks = tm // SUBLANES
    kern = functools.partial(_ab_in_kernel, n_tiles=n_tiles, ctx_tiles=ctx_tiles)
    width = w_in.shape[1]
    tok = lambda w: pl.BlockSpec((1, tm, w), lambda b, i: (b, i, 0))
    x_specs, x_args = [], []
    for rows, index in ((tm, lambda i: i), (SUBLANES, lambda i: i * halo_blocks - 1),
                        (SUBLANES, lambda i: (i + 1) * halo_blocks)):
        specs, args = _stream_windows(stream, rows, ctx_len, index)
        x_specs += specs
        x_args += args
    return pl.pallas_call(
        kern,
        grid=(B, n_tiles),
        in_specs=x_specs + [
            pl.BlockSpec((1, 1, 3, D_MODEL), lambda b, i: (b, (i >= ctx_tiles).astype(jnp.int32), 0, 0)),
            pl.BlockSpec((D_MODEL, width), lambda b, i: (0, 0)),
            pl.BlockSpec((3, B_WIDTH), lambda b, i: (0, 0)),
            pl.BlockSpec((tm, LANES), lambda b, i: (i, 0)),
            pl.BlockSpec((tm, LANES), lambda b, i: (i, 0)),
        ],
        out_specs=[tok(A_WIDTH), tok(A_KV_WIDTH), tok(A_KV_WIDTH), tok(A_WIDTH), tok(B_WIDTH)],
        out_shape=[
            jax.ShapeDtypeStruct((B, L, A_WIDTH), F32),
            jax.ShapeDtypeStruct((B, L, A_KV_WIDTH), F32),
            jax.ShapeDtypeStruct((B, L, A_KV_WIDTH), F32),
            jax.ShapeDtypeStruct((B, L, A_WIDTH), F32),
            jax.ShapeDtypeStruct((B, L, B_WIDTH), F32),
        ],
        compiler_params=_params(2),
        name="ab_in",
    )(*x_args, modsel, w_in, conv_w, cos_t, sin_t)


def _ab_attn_kernel(q_ref, kx_ref, kp_ref, kc_ref, kn_ref, vx_ref, vp_ref, vc_ref, vn_ref,
                    sga_ref, obg_ref, xc_ref, xl_ref, mod_ref, sink_ref, w_ref, g_ref, b_ref, o_ref,
                    *, n_blk, ctx_blk):
    i = pl.program_id(1)
    W = WINDOW
    n_sub = q_ref.shape[1] // W
    lc = kx_ref.shape[1]
    half = LANES // 2
    group = A_HEADS // A_KV_HEADS

    k_ctx, v_ctx = kx_ref[0].astype(BF16), vx_ref[0].astype(BF16)
    k_loc = jnp.concatenate([kp_ref[0], kc_ref[0], kn_ref[0]], axis=0).astype(BF16)
    v_loc = jnp.concatenate([vp_ref[0], vc_ref[0], vn_ref[0]], axis=0).astype(BF16)

    is_lat = i * n_sub >= ctx_blk
    ri = lax.broadcasted_iota(jnp.int32, (W, W), 0)
    ci = lax.broadcasted_iota(jnp.int32, (W, W), 1)
    neg = jnp.float32(-jnp.inf)
    lane = lax.broadcasted_iota(jnp.int32, (W, LANES), 1)
    low = lane < half

    def logits(sb, kvh):
        blk = i * n_sub + sb
        prev_ok = jnp.logical_and(is_lat, blk - 1 >= ctx_blk)
        next_ok = jnp.logical_and(is_lat, blk + 1 <= n_blk - 1)
        bias = jnp.concatenate([
            jnp.where(jnp.logical_and(prev_ok, ci >= ri), 0.0, neg),
            jnp.where(jnp.logical_and(is_lat, ci >= 0), 0.0, neg),
            jnp.where(jnp.logical_and(next_ok, ci <= ri), 0.0, neg),
        ], axis=1)
        bias = jnp.concatenate([bias] * group, axis=0)
        on_half = low if kvh == 0 else jnp.logical_not(low)
        stack, sinks = [], []
        for g in range(group):
            hd = kvh * group + g
            qh = q_ref[0, sb * W:(sb + 1) * W, (hd // 2) * LANES:(hd // 2 + 1) * LANES]
            if hd % 2 != kvh:
                qh = pltpu.roll(qh, half, axis=1)
            stack.append(jnp.where(on_half, qh, 0.0))
            sinks.append(jnp.broadcast_to(sink_ref[hd:hd + 1, :] * LOG2_E, (W, LANES)))
        k_sb = jnp.concatenate([k_ctx, k_loc[sb * W:(sb + 3) * W]], axis=0)
        s = _dot_nt(jnp.concatenate(stack, axis=0).astype(BF16), k_sb)
        return jnp.concatenate([s[:, :lc], s[:, lc:] + bias], axis=1), jnp.concatenate(sinks, axis=0)

    def attend(sb, s, sk):
        v_sb = jnp.concatenate([v_ctx, v_loc[sb * W:(sb + 3) * W]], axis=0)
        m = jnp.maximum(jnp.max(s, axis=-1, keepdims=True), sk)
        e = jnp.exp2(s - jnp.concatenate([m] * (s.shape[1] // LANES), axis=1))
        denom = jnp.sum(e, axis=-1, keepdims=True) + jnp.exp2(sk - m)
        return _dot(e.astype(BF16), v_sb) / denom

    stacks = [(sb, kvh) for sb in range(n_sub) for kvh in range(A_KV_HEADS)]
    pieces = {}
    nxt = logits(*stacks[0])
    for idx, (sb, kvh) in enumerate(stacks):
        cur = nxt
        if idx + 1 < len(stacks):
            nxt = logits(*stacks[idx + 1])
        pv = attend(sb, *cur)
        for g in range(group):
            hd = kvh * group + g
            piece = pv[g * W:(g + 1) * W]
            pieces[sb, hd] = piece if hd % 2 == kvh else pltpu.roll(piece, half, axis=1)
    o_a = jnp.concatenate([
        jnp.concatenate([jnp.where(low, pieces[sb, 2 * j], pieces[sb, 2 * j + 1]) for j in range(A_HEADS // 2)], axis=1)
        for sb in range(n_sub)], axis=0)

    y_in = jnp.concatenate([o_a * sga_ref[0], obg_ref[0]], axis=1).astype(BF16)
    y = _dot(y_in, w_ref[...])
    gate = mod_ref[0, 0, 2:3, :]
    x = jnp.where(is_lat, xl_ref[0], xc_ref[0])
    o_ref[0] = _layer_norm(DEEPNORM_ALPHA * x + gate * y, g_ref[...], b_ref[...])


def _ab_attn(stream, modsel, q, k, v, sga, obg, sink_b, w_out, ln_g, ln_b, ctx_len):
    B, L = stream[0].shape[0], _stream_len(stream)
    T = ROW_TILE
    W = WINDOW
    n_sub = T // W
    n_blk = L // W
    ctx_blk = ctx_len // W
    kern = functools.partial(_ab_attn_kernel, n_blk=n_blk, ctx_blk=ctx_blk)
    tok = lambda w: pl.BlockSpec((1, T, w), lambda b, i: (b, i, 0))
    prev = pl.BlockSpec((1, W, A_KV_WIDTH), lambda b, i: (b, jnp.maximum(i * n_sub - 1, 0), 0))
    nxt = pl.BlockSpec((1, W, A_KV_WIDTH), lambda b, i: (b, jnp.minimum((i + 1) * n_sub, n_blk - 1), 0))
    ctxs = pl.BlockSpec((1, ctx_len, A_KV_WIDTH), lambda b, i: (b, 0, 0))
    row = pl.BlockSpec((1, D_MODEL), lambda b, i: (0, 0))
    x_specs, x_args = _stream_windows(stream, T, ctx_len, lambda i: i)
    return pl.pallas_call(
        kern,
        grid=(B, L // T),
        in_specs=[
            tok(A_WIDTH),
            ctxs, prev, tok(A_KV_WIDTH), nxt,
            ctxs, prev, tok(A_KV_WIDTH), nxt,
            tok(A_WIDTH), tok(B_WIDTH), *x_specs,
            pl.BlockSpec((1, 1, 3, D_MODEL), lambda b, i: (b, (i * n_sub >= ctx_blk).astype(jnp.int32), 0, 0)),
            pl.BlockSpec((A_HEADS, LANES), lambda b, i: (0, 0)),
            pl.BlockSpec((D_MODEL, D_MODEL), lambda b, i: (0, 0)),
            row, row,
        ],
        out_specs=tok(D_MODEL),
        out_shape=jax.ShapeDtypeStruct((B, L, D_MODEL), F32),
        compiler_params=_params(2),
        name="ab_attn_out",
    )(q, k, k, k, k, v, v, v, v, sga, obg, *x_args, modsel, sink_b, w_out, ln_g, ln_b)


def _forget_gate(z, lb):
    e = jnp.exp(-jnp.abs(z))
    return jnp.where(z >= 0, 1.0 + lb * e, lb + e) / (1.0 + e)


def _c_in_kernel(x_ref, mod_ref, w_ref, lb_ref, q_ref, ff_ref, fb_ref, v_ref, sg_ref):
    h = _modulate(x_ref[0], mod_ref).astype(BF16)

    def proj(idx):
        return _dot(h, w_ref[:, idx * C_WIDTH:(idx + 1) * C_WIDTH])

    def put(ref, val):
        tm = val.shape[0]
        for hd in range(C_HEADS):
            g, hh = divmod(hd, HEAD_GROUP)
            ref[0, g, pl.ds(hh, tm, stride=HEAD_GROUP), :] = val[:, hd * LANES:(hd + 1) * LANES]

    put(q_ref, _silu(proj(0)) * (C_KEY_DIM ** -0.5))
    put(v_ref, proj(3))
    sg_ref[0] = _silu(proj(4)).astype(sg_ref.dtype)
    for d, f_ref in enumerate((ff_ref, fb_ref)):
        put(f_ref, _forget_gate(proj(1 + d), jnp.clip(lb_ref[d:d + 1, :], 0.0, 1.0)))


def _c_in(xcat, modsel, w_in, lb, ctx_len):
    B, L, _ = xcat.shape
    tm = ROW_TILE
    n_tiles = L // tm
    ctx_tiles = ctx_len // tm
    width = w_in.shape[1]
    groups = C_HEADS // HEAD_GROUP
    hm = pl.BlockSpec((1, groups, tm * HEAD_GROUP, LANES), lambda b, i: (b, 0, i, 0))
    hm_shape = jax.ShapeDtypeStruct((B, groups, L * HEAD_GROUP, LANES), F32)
    return pl.pallas_call(
        _c_in_kernel,
        grid=(B, n_tiles),
        in_specs=[
            pl.BlockSpec((1, tm, D_MODEL), lambda b, i: (b, i, 0)),
            pl.BlockSpec((1, 1, 3, D_MODEL), lambda b, i: (b, (i >= ctx_tiles).astype(jnp.int32), 0, 0)),
            pl.BlockSpec((D_MODEL, width), lambda b, i: (0, 0)),
            pl.BlockSpec((2, C_WIDTH), lambda b, i: (0, 0)),
        ],
        out_specs=[hm, hm, hm, hm, pl.BlockSpec((1, tm, C_WIDTH), lambda b, i: (b, i, 0))],
        out_shape=[hm_shape] * 4 + [jax.ShapeDtypeStruct((B, L, C_WIDTH), BF16)],
        compiler_params=_params(2),
        name="c_in",
    )(xcat, modsel, w_in, lb)


def _query_blocks(n, m, reverse):
    return [p for p in range(0, n, m) if ((p // m) % 2 == 1) != reverse]


N_SCAN_OPERANDS = len(SCAN_LEVELS) + 3


SCAN_CHUNK = 16
LOOKAHEAD = 2


def _scan_elementwise(q_ref, f_ref, v_ref, reverse, out_ref, decay_ref):
    assert SUBLANES == 2 * HEAD_GROUP
    n_rows, width = q_ref.shape
    chunk_rows = SCAN_CHUNK * HEAD_GROUP
    grouped = lambda t: t.reshape(t.shape[0] // SUBLANES, SUBLANES, width)
    low1 = lax.broadcasted_iota(jnp.int32, (1, SUBLANES, width), 1) < HEAD_GROUP
    swap = lambda t: pltpu.roll(t, HEAD_GROUP, axis=1)

    def block_total(t):
        return jnp.where(low1, t, swap(t)) if reverse else jnp.where(low1, swap(t), t)

    def double(q, k, run, rest, g):
        z, new_run, new_rest = [], [], []
        for a in range(0, run.shape[0], 2 * g):
            lo, hi = slice(a, a + g), slice(a + g, a + 2 * g)
            first, last = (hi, lo) if reverse else (lo, hi)
            edge = (lambda blk: run[blk.start:blk.start + 1]) if reverse else (lambda blk: run[blk.stop - 1:blk.stop])
            parts = {
                first: (k[first] * rest[first], run[first], rest[first] * block_total(edge(last))),
                last: (q[last] * run[last], run[last] * block_total(edge(first)), rest[last]),
            }
            for blk in (lo, hi):
                z.append(parts[blk][0])
                new_run.append(parts[blk][1])
                new_rest.append(parts[blk][2])
        return tuple(jnp.concatenate(t, axis=0) for t in (z, new_run, new_rest))

    def put(idx, rows, t):
        out_ref[idx, rows, :] = t.reshape(t.shape[0] * SUBLANES, width)

    low_levels = [m for m in SCAN_LEVELS if m < SCAN_CHUNK]
    n_low, n_lvl = len(low_levels), len(SCAN_LEVELS)
    chunks = [slice(r0, r0 + chunk_rows) for r0 in range(0, n_rows, chunk_rows)]
    q_run, k_rest, total = [], [], []
    for rows in chunks:
        q, f, v = grouped(q_ref[rows, :]), grouped(f_ref[rows, :]), grouped(v_ref[rows, :])
        k = 1.0 - f
        put(n_lvl + 2, rows, jnp.sum(q * k, axis=-1, keepdims=True) * v)
        second = low1 if reverse else jnp.logical_not(low1)
        other = swap(f)
        put(0, rows, jnp.where(second, q * f, k))
        run = jnp.where(second, f * other, f)
        rest = jnp.where(second, 1.0, other)
        for lvl, m in enumerate(low_levels[1:], start=1):
            z, run, rest = double(q, k, run, rest, m // 2)
            put(lvl, rows, z)
        q_run.append(q * run)
        k_rest.append(k * rest)
        total.append(block_total(run[0:1] if reverse else run[run.shape[0] - 1:]))

    mul = lambda a, b: b if a is None else a * b
    before, after = [None] * len(chunks), [None] * len(chunks)
    for lvl, m in enumerate(SCAN_LEVELS[n_low:], start=n_low):
        g = m // SCAN_CHUNK
        new_total = []
        for a in range(0, len(chunks), 2 * g):
            lo, hi = list(range(a, a + g)), list(range(a + g, a + 2 * g))
            first, last = (hi, lo) if reverse else (lo, hi)
            t_first, t_last = total[first[0] // g], total[last[0] // g]
            for c in first:
                put(lvl, chunks[c], mul(after[c], k_rest[c]))
                after[c] = mul(after[c], t_last)
            for c in last:
                put(lvl, chunks[c], mul(before[c], q_run[c]))
                before[c] = mul(before[c], t_first)
            new_total.append(t_first * t_last)
        total = new_total
    for c, rows in enumerate(chunks):
        put(n_lvl, rows, mul(before[c], q_run[c]))
        put(n_lvl + 1, rows, mul(after[c], k_rest[c]))
    decay_ref[...] = total[0][0]


def _scan_products(load, v, decay, state_t, mask_ref, d):
    n = v.shape[0]
    reverse = d == 1
    n_lvl = len(SCAN_LEVELS)
    blocks = [None] * (n // SUBLANES)

    def add_rows(start, term):
        for r in range(0, term.shape[0], SUBLANES):
            b = (start + r) // SUBLANES
            piece = term[r:r + SUBLANES]
            blocks[b] = piece if blocks[b] is None else blocks[b] + piece

    for lvl, m in enumerate(SCAN_LEVELS):
        z = load(lvl)
        if m < SUBLANES:
            zb = z.astype(BF16)
            add_rows(0, _dot_nt(zb, zb) * mask_ref[d, lvl])
        else:
            starts = _query_blocks(n, m, reverse)
            zq = jnp.concatenate([z[p:p + m] for p in starts], axis=0).astype(BF16)
            mask = jnp.concatenate([mask_ref[d, lvl, p:p + m, :] for p in starts], axis=0)
            term = _dot_nt(zq, z.astype(BF16)) * mask
            for idx, p in enumerate(starts):
                add_rows(p, term[idx * m:(idx + 1) * m])
    weights = jnp.concatenate(blocks, axis=0).astype(BF16)
    q_in, k_out, own = load(n_lvl).astype(BF16), load(n_lvl + 1).astype(BF16), load(n_lvl + 2)
    o_partial = _dot_nt(q_in, state_t.astype(BF16)) + own
    new_state = state_t * decay + _dot(v.T.astype(BF16), k_out)
    return weights, v.astype(BF16), o_partial, new_state


def _c_scan_kernel(qf_ref, ff_ref, vf_ref, qb_ref, fb_ref, vb_ref, mask_ref, of_ref, ob_ref,
                   st_ref, op0_ref, op1_ref, decay0_ref, decay1_ref):
    @pl.when(pl.program_id(1) == 0)
    def _():
        st_ref[...] = jnp.zeros(st_ref.shape, F32)

    T = SCAN_TILE
    n_tiles = of_ref.shape[2] // T
    dirs = ((qf_ref, ff_ref, vf_ref, of_ref), (qb_ref, fb_ref, vb_ref, ob_ref))
    units = [(n_tiles - 1 - t if d == 1 else t, g, d)
             for t in range(n_tiles) for g in range(C_HEADS // HEAD_GROUP) for d in range(2)]
    op_refs, decay_refs = (op0_ref, op1_ref), (decay0_ref, decay1_ref)

    def elementwise(idx):
        tile, g, d = units[idx]
        q_r, f_r, v_r, _ = dirs[d]
        slot = idx % 2
        rows = pl.ds(tile * T * HEAD_GROUP, T * HEAD_GROUP)
        _scan_elementwise(q_r.at[0, g, rows], f_r.at[0, g, rows], v_r.at[0, g, rows], d == 1,
                          op_refs[slot], decay_refs[slot])

    def finish(job):
        (tile, d, hd), (weights, v_bf, o_partial, _) = job
        o_r = dirs[d][3]
        o_r[0, hd, tile * T:(tile + 1) * T, :] = (o_partial + _dot(weights, v_bf)).astype(o_r.dtype)

    elementwise(0)
    pending = []
    for idx, (tile, g, d) in enumerate(units):
        if idx + 1 < len(units):
            elementwise(idx + 1)
        slot = idx % 2
        for hh in range(HEAD_GROUP):
            hd = g * HEAD_GROUP + hh
            rows = pl.ds(hh, T, stride=HEAD_GROUP)
            load = lambda i, rows=rows, slot=slot: op_refs[slot][i, rows, :]
            v_rows = pl.ds(tile * T * HEAD_GROUP + hh, T, stride=HEAD_GROUP)
            job = ((tile, d, hd), _scan_products(load, dirs[d][2][0, g, v_rows, :], decay_refs[slot][hh:hh + 1, :],
                                                 st_ref[d, hd], mask_ref, d))
            st_ref[d, hd] = job[1][3]
            pending.append(job)
            if len(pending) > LOOKAHEAD:
                finish(pending.pop(0))
    for job in pending:
        finish(job)


def _scan_masks():
    n = SCAN_TILE
    c = np.arange(n)[:, None]
    s = np.arange(n)[None, :]
    fwd = []
    for m in SCAN_LEVELS:
        fwd.append(((c // (2 * m) == s // (2 * m)) & (c % (2 * m) >= m) & (s % (2 * m) < m)).astype(np.float32))
    fwd = np.stack(fwd)
    return jnp.asarray(np.stack([fwd, fwd.transpose(0, 2, 1)]))


def _c_scan(q, ff, fb, v, ctx_len):
    B, groups, rows, _ = q.shape
    H = groups * HEAD_GROUP
    L = rows // HEAD_GROUP
    T = SCAN_TILE
    step = ROW_TILE
    assert step % T == 0 and L % step == 0 and ctx_len % step == 0
    n = L // step
    nc = ctx_len // step
    masks = _scan_masks()
    back = lambda j: jnp.where(j < nc, nc - 1 - j, n - 1 - (j - nc))
    fwd_in = pl.BlockSpec((1, groups, step * HEAD_GROUP, LANES), lambda b, j: (b, 0, j, 0))
    bwd_in = pl.BlockSpec((1, groups, step * HEAD_GROUP, LANES), lambda b, j: (b, 0, back(j), 0))
    fwd = pl.BlockSpec((1, H, step, LANES), lambda b, j: (b, 0, j, 0))
    bwd = pl.BlockSpec((1, H, step, LANES), lambda b, j: (b, 0, back(j), 0))
    hm_shape = jax.ShapeDtypeStruct((B, H, L, LANES), BF16)
    return pl.pallas_call(
        _c_scan_kernel,
        grid=(B, n),
        in_specs=[fwd_in, fwd_in, fwd_in, bwd_in, bwd_in, bwd_in,
                  pl.BlockSpec(masks.shape, lambda b, j: (0, 0, 0, 0))],
        out_specs=[fwd, bwd],
        out_shape=[hm_shape, hm_shape],
        scratch_shapes=[pltpu.VMEM((2, H, C_VAL_DIM, C_KEY_DIM), F32),
                        pltpu.VMEM((N_SCAN_OPERANDS, T * HEAD_GROUP, LANES), F32),
                        pltpu.VMEM((N_SCAN_OPERANDS, T * HEAD_GROUP, LANES), F32),
                        pltpu.VMEM((SUBLANES, LANES), F32),
                        pltpu.VMEM((SUBLANES, LANES), F32)],
        compiler_params=_params(2),
        name="c_scan",
    )(q, ff, v, q, fb, v, masks)


def _c_out_kernel(of_ref, ob_ref, sg_ref, x_ref, mod_ref, gn_ref, w_ref, g_ref, b_ref, o_ref):
    cols = []
    for hd in range(C_HEADS):
        o = of_ref[0, hd].astype(F32) + ob_ref[0, hd].astype(F32)
        ms = jnp.mean(o * o, axis=-1, keepdims=True)
        cols.append(o * lax.rsqrt(ms + RMS_EPS) * gn_ref[...])
    y_in = (jnp.concatenate(cols, axis=1) * sg_ref[0]).astype(BF16)
    y = _dot(y_in, w_ref[...])
    gate = mod_ref[0, 0, 2:3, :]
    o_ref[0] = _layer_norm(DEEPNORM_ALPHA * x_ref[0] + gate * y, g_ref[...], b_ref[...])


def _c_out(xcat, modsel, o_f, o_b, sg, gnorm, w_out, ln_g, ln_b, ctx_len, latent_only):
    B, L, _ = xcat.shape
    tm = ROW_TILE
    ctx_tiles = ctx_len // tm
    skip = ctx_tiles if latent_only else 0
    n_tiles = L // tm - skip
    hm = pl.BlockSpec((1, C_HEADS, tm, LANES), lambda b, i: (b, 0, i + skip, 0))
    tok = pl.BlockSpec((1, tm, D_MODEL), lambda b, i: (b, i + skip, 0))
    row = pl.BlockSpec((1, D_MODEL), lambda b, i: (0, 0))
    return pl.pallas_call(
        _c_out_kernel,
        grid=(B, n_tiles),
        in_specs=[
            hm, hm, tok, tok,
            pl.BlockSpec((1, 1, 3, D_MODEL), lambda b, i: (b, (i + skip >= ctx_tiles).astype(jnp.int32), 0, 0)),
            pl.BlockSpec((1, LANES), lambda b, i: (0, 0)),
            pl.BlockSpec((D_MODEL, D_MODEL), lambda b, i: (0, 0)),
            row, row,
        ],
        out_specs=pl.BlockSpec((1, tm, D_MODEL), lambda b, i: (b, i, 0)),
        out_shape=jax.ShapeDtypeStruct((B, n_tiles * tm, D_MODEL), F32),
        compiler_params=_params(2),
        name="c_out",
    )(o_f, o_b, sg, xcat, modsel, gnorm, w_out, ln_g, ln_b)


def _rope_tables(seq, ctx_len):
    t = jnp.arange(seq)
    freqs = ROPE_BASE ** (-jnp.arange(16, dtype=F32) / 16)
    ang_r = (t // GRID_W).astype(F32)[:, None] * freqs[None, :]
    ang_c = (t % GRID_W).astype(F32)[:, None] * freqs[None, :]
    cos = jnp.concatenate([jnp.cos(ang_r)] * 2 + [jnp.cos(ang_c)] * 2, axis=-1)
    sin = jnp.concatenate([-jnp.sin(ang_r), jnp.sin(ang_r), -jnp.sin(ang_c), jnp.sin(ang_c)], axis=-1)
    cos = jnp.concatenate([jnp.ones((ctx_len, 64), F32), cos], axis=0)
    sin = jnp.concatenate([jnp.zeros((ctx_len, 64), F32), sin], axis=0)
    return jnp.tile(cos, (1, 2)), jnp.tile(sin, (1, 2))


def kernel(x, c, ctx, c_ctx, w_ada, b_ada, ln_g, ln_b, w_in_ab, w_out_ab, sink_ab, conv_ab,
           w_in_c, w_out_c, lb_c, gnorm_c):
    B, seq, _ = x.shape
    ctx_len = ctx.shape[1]
    depth = w_ada.shape[0]
    assert seq % ROW_TILE == 0 and ctx_len % ROW_TILE == 0 and B <= 8

    s = jnp.concatenate([c, c_ctx[None, :], jnp.zeros((16 - B - 1, D_MODEL), F32)], axis=0)
    mod_all = _ada(s, w_ada, b_ada)
    mod_lat = mod_all[:, :B].reshape(depth, B, 1, 3, D_MODEL)
    mod_ctx = jnp.broadcast_to(mod_all[:, B].reshape(depth, 1, 1, 3, D_MODEL), mod_lat.shape)
    modsel = jnp.concatenate([mod_ctx, mod_lat], axis=2)

    lb_p = jax.nn.softmax(lb_c.astype(F32), axis=1)
    lb_all = jnp.cumsum(lb_p, axis=1) - lb_p[:, :1]

    cos_t, sin_t = _rope_tables(seq, ctx_len)
    xcat = None
    for l in range(depth):
        j = l // 2
        g_row, b_row = ln_g[l][None, :], ln_b[l][None, :]
        if l % 2 == 0:
            stream = (ctx, x, ctx_len) if xcat is None else (xcat, xcat, 0)
            q, k, v, sga, obg = _ab_in(stream, modsel[l], w_in_ab[j].astype(BF16), conv_ab[j], cos_t, sin_t, ctx_len)
            sink_b = jnp.broadcast_to(sink_ab[j].astype(F32)[:, None], (A_HEADS, LANES))
            xcat = _ab_attn(stream, modsel[l], q, k, v, sga, obg, sink_b, w_out_ab[j].astype(BF16),
                            g_row, b_row, ctx_len)
        else:
            q, ff, fb, v, sg = _c_in(xcat, modsel[l], w_in_c[j].astype(BF16), lb_all[:, j], ctx_len)
            o_f, o_b = _c_scan(q, ff, fb, v, ctx_len)
            xcat = _c_out(xcat, modsel[l], o_f, o_b, sg, gnorm_c[j][None, :], w_out_c[j].astype(BF16),
                          g_row, b_row, ctx_len, latent_only=(l == depth - 1))
    return xcat if depth % 2 == 0 else xcat[:, ctx_len:]
```

```python
import functools

import numpy as np
import jax
import jax.numpy as jnp
from jax import lax
from jax.experimental import pallas as pl
from jax.experimental.pallas import tpu as pltpu

F32 = jnp.float32
BF16 = jnp.bfloat16

D_MODEL = 1024
DEPTH = 4
GRID_W = 64
A_HEADS = 8
A_KV_HEADS = 2
A_HEAD_DIM = 64
A_WIDTH = A_HEADS * A_HEAD_DIM
A_KV_WIDTH = A_KV_HEADS * A_HEAD_DIM
WINDOW = 128
ROPE_BASE = 10000.0
B_WIDTH = D_MODEL // 2
C_HEADS = 8
C_KEY_DIM = 128
C_VAL_DIM = D_MODEL // C_HEADS
C_WIDTH = C_HEADS * C_KEY_DIM
DEEPNORM_ALPHA = (2 * DEPTH) ** 0.25
LN_EPS = 1e-5
RMS_EPS = 1e-6
LOG2_E = 1.4426950408889634

LANES = 128
SUBLANES = 8
ROW_TILE = 256
SCAN_TILE = 128
BATCH_PAIR = 2
SCAN_LEVELS = (1, 2, 4, 8, 16, 32, 64)
HEAD_GROUP = 4
VMEM_LIMIT = 56 * 1024 * 1024


def _params(n_axes):
    return pltpu.CompilerParams(dimension_semantics=("arbitrary",) * n_axes, vmem_limit_bytes=VMEM_LIMIT)


def _dot(a, b):
    return jnp.dot(a, b, preferred_element_type=F32)


def _dot_nt(a, b):
    return lax.dot_general(a, b, (((1,), (1,)), ((), ())), preferred_element_type=F32)


def _silu(t):
    return t * jax.nn.sigmoid(t)


def _modulate(x, mod_ref, bb=0):
    shift = mod_ref[bb, 0, 0:1, :]
    scale = mod_ref[bb, 0, 1:2, :]
    return x * (1 + scale) + shift


def _layer_norm(r, g, b):
    mu = jnp.mean(r, axis=-1, keepdims=True)
    rc = r - mu
    var = jnp.mean(rc * rc, axis=-1, keepdims=True)
    return rc * lax.rsqrt(var + LN_EPS) * g + b


def _ada_kernel(s_ref, w_ref, b_ref, o_ref):
    s = _silu(s_ref[...]).astype(BF16)
    o_ref[0] = _dot(s, w_ref[0].astype(BF16)) + b_ref[0]


def _ada(s, w_ada, b_ada):
    depth = w_ada.shape[0]
    rows = s.shape[0]
    n_col = 3 * D_MODEL // D_MODEL
    return pl.pallas_call(
        _ada_kernel,
        grid=(depth, n_col),
        in_specs=[
            pl.BlockSpec((rows, D_MODEL), lambda l, n: (0, 0)),
            pl.BlockSpec((1, D_MODEL, D_MODEL), lambda l, n: (l, 0, n)),
            pl.BlockSpec((1, 1, D_MODEL), lambda l, n: (l, 0, n)),
        ],
        out_specs=pl.BlockSpec((1, rows, D_MODEL), lambda l, n: (l, 0, n)),
        out_shape=jax.ShapeDtypeStruct((depth, rows, 3 * D_MODEL), F32),
        compiler_params=_params(2),
        name="ada_mod",
    )(s, w_ada, b_ada.reshape(depth, 1, 3 * D_MODEL))


def _rope(t, cos, sin_signed, lane_lo):
    partner = jnp.where(lane_lo, pltpu.roll(t, LANES - 16, axis=1), pltpu.roll(t, 16, axis=1))
    return t * cos + partner * sin_signed


def _ab_in_kernel(xc_ref, xl_ref, xpc_ref, xpl_ref, xnc_ref, xnl_ref, mod_ref, w_ref, cw_ref, cos_ref, sin_ref,
                  q_ref, k_ref, v_ref, sga_ref, obg_ref, *, n_tiles, ctx_tiles):
    i = pl.program_id(1)
    tm = xc_ref.shape[1]
    is_ctx = i < ctx_tiles
    h = _modulate(jnp.where(is_ctx, xc_ref[0], xl_ref[0]), mod_ref).astype(BF16)

    def proj(lhs, lo, hi):
        return _dot(lhs, w_ref[:, lo:hi])

    o_q, o_k, o_v, o_ga = 0, A_WIDTH, A_WIDTH + A_KV_WIDTH, A_WIDTH + 2 * A_KV_WIDTH
    o_xb = o_ga + A_WIDTH
    o_bg, o_cg, o_gb = o_xb + B_WIDTH, o_xb + 2 * B_WIDTH, o_xb + 3 * B_WIDTH

    cos = cos_ref[...]
    sin_signed = sin_ref[...]
    lane = lax.broadcasted_iota(jnp.int32, (tm, LANES), 1)
    lane_lo = (lane % 32) < 16
    q = proj(h, o_q, o_k)
    for j in range(A_WIDTH // LANES):
        qj = _rope(q[:, j * LANES:(j + 1) * LANES], cos, sin_signed, lane_lo)
        q_ref[0, :, j * LANES:(j + 1) * LANES] = qj * (A_HEAD_DIM ** -0.5 * LOG2_E)
    k_ref[0] = _rope(proj(h, o_k, o_v), cos, sin_signed, lane_lo)
    v_ref[0] = proj(h, o_v, o_ga)
    sga_ref[0] = _silu(proj(h, o_ga, o_xb))

    u = proj(h, o_cg, o_gb) * proj(h, o_xb, o_bg)
    halo = jnp.concatenate([jnp.where(is_ctx, xpc_ref[0], xpl_ref[0]), jnp.where(is_ctx, xnc_ref[0], xnl_ref[0])],
                           axis=0)
    hh = _modulate(halo, mod_ref).astype(BF16)
    uh = proj(hh, o_cg, o_gb) * proj(hh, o_xb, o_bg)
    left_ok = jnp.logical_and(i != 0, i != ctx_tiles)
    right_ok = jnp.logical_and(i != ctx_tiles - 1, i != n_tiles - 1)
    u_left = jnp.where(left_ok, uh[SUBLANES - 1:SUBLANES], 0.0)
    u_right = jnp.where(right_ok, uh[SUBLANES:SUBLANES + 1], 0.0)
    row = lax.broadcasted_iota(jnp.int32, u.shape, 0)
    u_prev = jnp.where(row == 0, u_left, pltpu.roll(u, 1, axis=0))
    u_next = jnp.where(row == tm - 1, u_right, pltpu.roll(u, tm - 1, axis=0))
    conv = u_prev * cw_ref[0:1, :] + u * cw_ref[1:2, :] + u_next * cw_ref[2:3, :]
    obg_ref[0] = proj(h, o_bg, o_cg) * conv * _silu(proj(h, o_gb, o_gb + B_WIDTH))


def _stream_len(stream):
    return stream[1].shape[1] + stream[2]


def _stream_windows(stream, rows, ctx_len, index):
    ctx_arr, lat_arr, shift = stream
    ctx_blocks, n_blocks, skip = ctx_len // rows, _stream_len(stream) // rows, shift // rows
    specs = [
        pl.BlockSpec((1, rows, D_MODEL), lambda b, i: (b, jnp.clip(index(i), 0, ctx_blocks - 1), 0)),
        pl.BlockSpec((1, rows, D_MODEL), lambda b, i: (b, jnp.clip(index(i), ctx_blocks, n_blocks - 1) - skip, 0)),
    ]
    return specs, [ctx_arr, lat_arr]


def _ab_in(stream, modsel, w_in, conv_w, cos_t, sin_t, ctx_len):
    B, L = stream[0].shape[0], _stream_len(stream)
    tm = ROW_TILE
    n_tiles = L // tm
    ctx_tiles = ctx_len // tm
    halo_blocks = tm // SUBLANES
    kern = functools.partial(_ab_in_kernel, n_tiles=n_tiles, ctx_tiles=ctx_tiles)
    width = w_in.shape[1]
    tok = lambda w: pl.BlockSpec((1, tm, w), lambda b, i: (b, i, 0))
    x_specs, x_args = [], []
    for rows, index in ((tm, lambda i: i), (SUBLANES, lambda i: i * halo_blocks - 1),
                        (SUBLANES, lambda i: (i + 1) * halo_blocks)):
        specs, args = _stream_windows(stream, rows, ctx_len, index)
        x_specs += specs
        x_args += args
    return pl.pallas_call(
        kern,
        grid=(B, n_tiles),
        in_specs=x_specs + [
            pl.BlockSpec((1, 1, 3, D_MODEL), lambda b, i: (b, (i >= ctx_tiles).astype(jnp.int32), 0, 0)),
            pl.BlockSpec((D_MODEL, width), lambda b, i: (0, 0)),
            pl.BlockSpec((3, B_WIDTH), lambda b, i: (0, 0)),
            pl.BlockSpec((tm, LANES), lambda b, i: (i, 0)),
            pl.BlockSpec((tm, LANES), lambda b, i: (i, 0)),
        ],
        out_specs=[tok(A_WIDTH), tok(A_KV_WIDTH), tok(A_KV_WIDTH), tok(A_WIDTH), tok(B_WIDTH)],
        out_shape=[
            jax.ShapeDtypeStruct((B, L, A_WIDTH), F32),
            jax.ShapeDtypeStruct((B, L, A_KV_WIDTH), F32),
            jax.ShapeDtypeStruct((B, L, A_KV_WIDTH), F32),
            jax.ShapeDtypeStruct((B, L, A_WIDTH), F32),
            jax.ShapeDtypeStruct((B, L, B_WIDTH), F32),
        ],
        compiler_params=_params(2),
        name="ab_in",
    )(*x_args, modsel, w_in, conv_w, cos_t, sin_t)


def _ab_attn_kernel(q_ref, kx_ref, kp_ref, kc_ref, kn_ref, vx_ref, vp_ref, vc_ref, vn_ref,
                    sga_ref, obg_ref, xc_ref, xl_ref, mod_ref, sink_ref, w_ref, g_ref, b_ref, o_ref,
                    *, n_blk, ctx_blk):
    i = pl.program_id(1)
    W = WINDOW
    n_sub = q_ref.shape[1] // W
    lc = kx_ref.shape[1]
    half = LANES // 2
    group = A_HEADS // A_KV_HEADS

    k_ctx, v_ctx = kx_ref[0].astype(BF16), vx_ref[0].astype(BF16)
    k_loc = jnp.concatenate([kp_ref[0], kc_ref[0], kn_ref[0]], axis=0).astype(BF16)
    v_loc = jnp.concatenate([vp_ref[0], vc_ref[0], vn_ref[0]], axis=0).astype(BF16)

    is_lat = i * n_sub >= ctx_blk
    ri = lax.broadcasted_iota(jnp.int32, (W, W), 0)
    ci = lax.broadcasted_iota(jnp.int32, (W, W), 1)
    neg = jnp.float32(-jnp.inf)
    lane = lax.broadcasted_iota(jnp.int32, (W, LANES), 1)
    low = lane < half

    def logits(sb, kvh):
        blk = i * n_sub + sb
        prev_ok = jnp.logical_and(is_lat, blk - 1 >= ctx_blk)
        next_ok = jnp.logical_and(is_lat, blk + 1 <= n_blk - 1)
        bias = jnp.concatenate([
            jnp.where(jnp.logical_and(prev_ok, ci >= ri), 0.0, neg),
            jnp.where(jnp.logical_and(is_lat, ci >= 0), 0.0, neg),
            jnp.where(jnp.logical_and(next_ok, ci <= ri), 0.0, neg),
        ], axis=1)
        bias = jnp.concatenate([bias] * group, axis=0)
        on_half = low if kvh == 0 else jnp.logical_not(low)
        stack, sinks = [], []
        for g in range(group):
            hd = kvh * group + g
            qh = q_ref[0, sb * W:(sb + 1) * W, (hd // 2) * LANES:(hd // 2 + 1) * LANES]
            if hd % 2 != kvh:
                qh = pltpu.roll(qh, half, axis=1)
            stack.append(jnp.where(on_half, qh, 0.0))
            sinks.append(jnp.broadcast_to(sink_ref[hd:hd + 1, :] * LOG2_E, (W, LANES)))
        k_sb = jnp.concatenate([k_ctx, k_loc[sb * W:(sb + 3) * W]], axis=0)
        s = _dot_nt(jnp.concatenate(stack, axis=0).astype(BF16), k_sb)
        return jnp.concatenate([s[:, :lc], s[:, lc:] + bias], axis=1), jnp.concatenate(sinks, axis=0)

    def attend(sb, s, sk):
        v_sb = jnp.concatenate([v_ctx, v_loc[sb * W:(sb + 3) * W]], axis=0)
        m = jnp.maximum(jnp.max(s, axis=-1, keepdims=True), sk)
        e = jnp.exp2(s - jnp.concatenate([m] * (s.shape[1] // LANES), axis=1))
        denom = jnp.sum(e, axis=-1, keepdims=True) + jnp.exp2(sk - m)
        return _dot(e.astype(BF16), v_sb) / denom

    stacks = [(sb, kvh) for sb in range(n_sub) for kvh in range(A_KV_HEADS)]
    pieces = {}
    nxt = logits(*stacks[0])
    for idx, (sb, kvh) in enumerate(stacks):
        cur = nxt
        if idx + 1 < len(stacks):
            nxt = logits(*stacks[idx + 1])
        pv = attend(sb, *cur)
        for g in range(group):
            hd = kvh * group + g
            piece = pv[g * W:(g + 1) * W]
            pieces[sb, hd] = piece if hd % 2 == kvh else pltpu.roll(piece, half, axis=1)
    o_a = jnp.concatenate([
        jnp.concatenate([jnp.where(low, pieces[sb, 2 * j], pieces[sb, 2 * j + 1]) for j in range(A_HEADS // 2)], axis=1)
        for sb in range(n_sub)], axis=0)

    y_in = jnp.concatenate([o_a * sga_ref[0], obg_ref[0]], axis=1).astype(BF16)
    y = _dot(y_in, w_ref[...])
    gate = mod_ref[0, 0, 2:3, :]
    x = jnp.where(is_lat, xl_ref[0], xc_ref[0])
    o_ref[0] = _layer_norm(DEEPNORM_ALPHA * x + gate * y, g_ref[...], b_ref[...])


def _ab_attn(stream, modsel, q, k, v, sga, obg, sink_b, w_out, ln_g, ln_b, ctx_len):
    B, L = stream[0].shape[0], _stream_len(stream)
    T = ROW_TILE
    W = WINDOW
    n_sub = T // W
    n_blk = L // W
    ctx_blk = ctx_len // W
    kern = functools.partial(_ab_attn_kernel, n_blk=n_blk, ctx_blk=ctx_blk)
    tok = lambda w: pl.BlockSpec((1, T, w), lambda b, i: (b, i, 0))
    prev = pl.BlockSpec((1, W, A_KV_WIDTH), lambda b, i: (b, jnp.maximum(i * n_sub - 1, 0), 0))
    nxt = pl.BlockSpec((1, W, A_KV_WIDTH), lambda b, i: (b, jnp.minimum((i + 1) * n_sub, n_blk - 1), 0))
    ctxs = pl.BlockSpec((1, ctx_len, A_KV_WIDTH), lambda b, i: (b, 0, 0))
    row = pl.BlockSpec((1, D_MODEL), lambda b, i: (0, 0))
    x_specs, x_args = _stream_windows(stream, T, ctx_len, lambda i: i)
    return pl.pallas_call(
        kern,
        grid=(B, L // T),
        in_specs=[
            tok(A_WIDTH),
            ctxs, prev, tok(A_KV_WIDTH), nxt,
            ctxs, prev, tok(A_KV_WIDTH), nxt,
            tok(A_WIDTH), tok(B_WIDTH), *x_specs,
            pl.BlockSpec((1, 1, 3, D_MODEL), lambda b, i: (b, (i * n_sub >= ctx_blk).astype(jnp.int32), 0, 0)),
            pl.BlockSpec((A_HEADS, LANES), lambda b, i: (0, 0)),
            pl.BlockSpec((D_MODEL, D_MODEL), lambda b, i: (0, 0)),
            row, row,
        ],
        out_specs=tok(D_MODEL),
        out_shape=jax.ShapeDtypeStruct((B, L, D_MODEL), F32),
        compiler_params=_params(2),
        name="ab_attn_out",
    )(q, k, k, k, k, v, v, v, v, sga, obg, *x_args, modsel, sink_b, w_out, ln_g, ln_b)


def _forget_gate(z, lb):
    e = jnp.exp(-jnp.abs(z))
    return jnp.where(z >= 0, 1.0 + lb * e, lb + e) / (1.0 + e)


def _c_in_kernel(x_ref, mod_ref, w_ref, lb_ref, q_ref, ff_ref, fb_ref, v_ref, sg_ref):
    nb, tm = x_ref.shape[0], x_ref.shape[1]
    h = jnp.concatenate([_modulate(x_ref[bb], mod_ref, bb) for bb in range(nb)], axis=0).astype(BF16)

    def proj(idx):
        return _dot(h, w_ref[:, idx * C_WIDTH:(idx + 1) * C_WIDTH])

    def put(ref, val):
        for bb in range(nb):
            for hd in range(C_HEADS):
                g, hh = divmod(hd, HEAD_GROUP)
                ref[bb, g, pl.ds(hh, tm, stride=HEAD_GROUP), :] = val[bb * tm:(bb + 1) * tm, hd * LANES:(hd + 1) * LANES]

    for d, f_ref in enumerate((ff_ref, fb_ref)):
        put(f_ref, _forget_gate(proj(1 + d), jnp.clip(lb_ref[d:d + 1, :], 0.0, 1.0)))
    put(q_ref, _silu(proj(0)) * (C_KEY_DIM ** -0.5))
    sg_ref[...] = _silu(proj(4)).astype(sg_ref.dtype).reshape(sg_ref.shape)
    put(v_ref, proj(3))


def _c_in(xcat, modsel, w_in, lb, ctx_len):
    B, L, _ = xcat.shape
    tm = ROW_TILE
    nb = BATCH_PAIR if B % BATCH_PAIR == 0 else 1
    n_tiles = L // tm
    ctx_tiles = ctx_len // tm
    width = w_in.shape[1]
    groups = C_HEADS // HEAD_GROUP
    hm = pl.BlockSpec((nb, groups, tm * HEAD_GROUP, LANES), lambda b, i: (b, 0, i, 0))
    hm_shape = jax.ShapeDtypeStruct((B, groups, L * HEAD_GROUP, LANES), F32)
    return pl.pallas_call(
        _c_in_kernel,
        grid=(B // nb, n_tiles),
        in_specs=[
            pl.BlockSpec((nb, tm, D_MODEL), lambda b, i: (b, i, 0)),
            pl.BlockSpec((nb, 1, 3, D_MODEL), lambda b, i: (b, (i >= ctx_tiles).astype(jnp.int32), 0, 0)),
            pl.BlockSpec((D_MODEL, width), lambda b, i: (0, 0)),
            pl.BlockSpec((2, C_WIDTH), lambda b, i: (0, 0)),
        ],
        out_specs=[hm, hm, hm, hm, pl.BlockSpec((nb, tm, C_WIDTH), lambda b, i: (b, i, 0))],
        out_shape=[hm_shape] * 4 + [jax.ShapeDtypeStruct((B, L, C_WIDTH), BF16)],
        compiler_params=_params(2),
        name="c_in",
    )(xcat, modsel, w_in, lb)


def _query_blocks(n, m, reverse):
    return [p for p in range(0, n, m) if ((p // m) % 2 == 1) != reverse]


N_SCAN_OPERANDS = len(SCAN_LEVELS) + 3


SCAN_CHUNK = 16
LOOKAHEAD = 2


def _scan_elementwise(q_ref, f_ref, v_ref, reverse, out_ref, decay_ref):
    assert SUBLANES == 2 * HEAD_GROUP
    n_rows, width = q_ref.shape
    chunk_rows = SCAN_CHUNK * HEAD_GROUP
    grouped = lambda t: t.reshape(t.shape[0] // SUBLANES, SUBLANES, width)
    low1 = lax.broadcasted_iota(jnp.int32, (1, SUBLANES, width), 1) < HEAD_GROUP
    swap = lambda t: pltpu.roll(t, HEAD_GROUP, axis=1)

    def block_total(t):
        return jnp.where(low1, t, swap(t)) if reverse else jnp.where(low1, swap(t), t)

    def double(q, k, run, rest, g):
        z, new_run, new_rest = [], [], []
        for a in range(0, run.shape[0], 2 * g):
            lo, hi = slice(a, a + g), slice(a + g, a + 2 * g)
            first, last = (hi, lo) if reverse else (lo, hi)
            edge = (lambda blk: run[blk.start:blk.start + 1]) if reverse else (lambda blk: run[blk.stop - 1:blk.stop])
            parts = {
                first: (k[first] * rest[first], run[first], rest[first] * block_total(edge(last))),
                last: (q[last] * run[last], run[last] * block_total(edge(first)), rest[last]),
            }
            for blk in (lo, hi):
                z.append(parts[blk][0])
                new_run.append(parts[blk][1])
                new_rest.append(parts[blk][2])
        return tuple(jnp.concatenate(t, axis=0) for t in (z, new_run, new_rest))

    def put(idx, rows, t):
        out_ref[idx, rows, :] = t.reshape(t.shape[0] * SUBLANES, width)

    low_levels = [m for m in SCAN_LEVELS if m < SCAN_CHUNK]
    n_low, n_lvl = len(low_levels), len(SCAN_LEVELS)
    chunks = [slice(r0, r0 + chunk_rows) for r0 in range(0, n_rows, chunk_rows)]
    q_run, k_rest, total = [], [], []
    for rows in chunks:
        q, f, v = grouped(q_ref[rows, :]), grouped(f_ref[rows, :]), grouped(v_ref[rows, :])
        k = 1.0 - f
        put(n_lvl + 2, rows, jnp.sum(q * k, axis=-1, keepdims=True) * v)
        second = low1 if reverse else jnp.logical_not(low1)
        other = swap(f)
        put(0, rows, jnp.where(second, q * f, k))
        run = jnp.where(second, f * other, f)
        rest = jnp.where(second, 1.0, other)
        for lvl, m in enumerate(low_levels[1:], start=1):
            z, run, rest = double(q, k, run, rest, m // 2)
            put(lvl, rows, z)
        q_run.append(q * run)
        k_rest.append(k * rest)
        total.append(block_total(run[0:1] if reverse else run[run.shape[0] - 1:]))

    mul = lambda a, b: b if a is None else a * b
    before, after = [None] * len(chunks), [None] * len(chunks)
    for lvl, m in enumerate(SCAN_LEVELS[n_low:], start=n_low):
        g = m // SCAN_CHUNK
        new_total = []
        for a in range(0, len(chunks), 2 * g):
            lo, hi = list(range(a, a + g)), list(range(a + g, a + 2 * g))
            first, last = (hi, lo) if reverse else (lo, hi)
            t_first, t_last = total[first[0] // g], total[last[0] // g]
            for c in first:
                put(lvl, chunks[c], mul(after[c], k_rest[c]))
                after[c] = mul(after[c], t_last)
            for c in last:
                put(lvl, chunks[c], mul(before[c], q_run[c]))
                before[c] = mul(before[c], t_first)
            new_total.append(t_first * t_last)
        total = new_total
    for c, rows in enumerate(chunks):
        put(n_lvl, rows, mul(before[c], q_run[c]))
        put(n_lvl + 1, rows, mul(after[c], k_rest[c]))
    decay_ref[...] = total[0][0]


def _scan_products(load, v, decay, state_t, mask_ref, d):
    n = v.shape[0]
    reverse = d == 1
    n_lvl = len(SCAN_LEVELS)
    blocks = [None] * (n // SUBLANES)

    def add_rows(start, term):
        for r in range(0, term.shape[0], SUBLANES):
            b = (start + r) // SUBLANES
            piece = term[r:r + SUBLANES]
            blocks[b] = piece if blocks[b] is None else blocks[b] + piece

    for lvl, m in enumerate(SCAN_LEVELS):
        z = load(lvl)
        if m < SUBLANES:
            zb = z.astype(BF16)
            add_rows(0, _dot_nt(zb, zb) * mask_ref[d, lvl])
        else:
            starts = _query_blocks(n, m, reverse)
            zq = jnp.concatenate([z[p:p + m] for p in starts], axis=0).astype(BF16)
            mask = jnp.concatenate([mask_ref[d, lvl, p:p + m, :] for p in starts], axis=0)
            term = _dot_nt(zq, z.astype(BF16)) * mask
            for idx, p in enumerate(starts):
                add_rows(p, term[idx * m:(idx + 1) * m])
    weights = jnp.concatenate(blocks, axis=0).astype(BF16)
    q_in, k_out, own = load(n_lvl).astype(BF16), load(n_lvl + 1).astype(BF16), load(n_lvl + 2)
    o_partial = _dot_nt(q_in, state_t.astype(BF16)) + own
    new_state = state_t * decay + _dot(v.T.astype(BF16), k_out)
    return weights, v.astype(BF16), o_partial, new_state


def _c_scan_kernel(qf_ref, ff_ref, vf_ref, qb_ref, fb_ref, vb_ref, mask_ref, of_ref, ob_ref,
                   st_ref, op0_ref, op1_ref, decay0_ref, decay1_ref):
    @pl.when(pl.program_id(1) == 0)
    def _():
        st_ref[...] = jnp.zeros(st_ref.shape, F32)

    T = SCAN_TILE
    n_tiles = of_ref.shape[2] // T
    dirs = ((qf_ref, ff_ref, vf_ref, of_ref), (qb_ref, fb_ref, vb_ref, ob_ref))
    units = [(n_tiles - 1 - t if d == 1 else t, g, d)
             for t in range(n_tiles) for g in range(C_HEADS // HEAD_GROUP) for d in range(2)]
    op_refs, decay_refs = (op0_ref, op1_ref), (decay0_ref, decay1_ref)

    def elementwise(idx):
        tile, g, d = units[idx]
        q_r, f_r, v_r, _ = dirs[d]
        slot = idx % 2
        rows = pl.ds(tile * T * HEAD_GROUP, T * HEAD_GROUP)
        _scan_elementwise(q_r.at[0, g, rows], f_r.at[0, g, rows], v_r.at[0, g, rows], d == 1,
                          op_refs[slot], decay_refs[slot])

    def finish(job):
        (tile, d, hd), (weights, v_bf, o_partial, _) = job
        o_r = dirs[d][3]
        o_r[0, hd, tile * T:(tile + 1) * T, :] = (o_partial + _dot(weights, v_bf)).astype(o_r.dtype)

    elementwise(0)
    pending = []
    for idx, (tile, g, d) in enumerate(units):
        if idx + 1 < len(units):
            elementwise(idx + 1)
        slot = idx % 2
        for hh in range(HEAD_GROUP):
            hd = g * HEAD_GROUP + hh
            rows = pl.ds(hh, T, stride=HEAD_GROUP)
            load = lambda i, rows=rows, slot=slot: op_refs[slot][i, rows, :]
            v_rows = pl.ds(tile * T * HEAD_GROUP + hh, T, stride=HEAD_GROUP)
            job = ((tile, d, hd), _scan_products(load, dirs[d][2][0, g, v_rows, :], decay_refs[slot][hh:hh + 1, :],
                                                 st_ref[d, hd], mask_ref, d))
            st_ref[d, hd] = job[1][3]
            pending.append(job)
            if len(pending) > LOOKAHEAD:
                finish(pending.pop(0))
    for job in pending:
        finish(job)


def _scan_masks():
    n = SCAN_TILE
    c = np.arange(n)[:, None]
    s = np.arange(n)[None, :]
    fwd = []
    for m in SCAN_LEVELS:
        fwd.append(((c // (2 * m) == s // (2 * m)) & (c % (2 * m) >= m) & (s % (2 * m) < m)).astype(np.float32))
    fwd = np.stack(fwd)
    return jnp.asarray(np.stack([fwd, fwd.transpose(0, 2, 1)]))


def _c_scan(q, ff, fb, v, ctx_len):
    B, groups, rows, _ = q.shape
    H = groups * HEAD_GROUP
    L = rows // HEAD_GROUP
    T = SCAN_TILE
    step = ROW_TILE
    assert step % T == 0 and L % step == 0 and ctx_len % step == 0
    n = L // step
    nc = ctx_len // step
    masks = _scan_masks()
    back = lambda j: jnp.where(j < nc, nc - 1 - j, n - 1 - (j - nc))
    fwd_in = pl.BlockSpec((1, groups, step * HEAD_GROUP, LANES), lambda b, j: (b, 0, j, 0))
    bwd_in = pl.BlockSpec((1, groups, step * HEAD_GROUP, LANES), lambda b, j: (b, 0, back(j), 0))
    fwd = pl.BlockSpec((1, H, step, LANES), lambda b, j: (b, 0, j, 0))
    bwd = pl.BlockSpec((1, H, step, LANES), lambda b, j: (b, 0, back(j), 0))
    hm_shape = jax.ShapeDtypeStruct((B, H, L, LANES), BF16)
    return pl.pallas_call(
        _c_scan_kernel,
        grid=(B, n),
        in_specs=[fwd_in, fwd_in, fwd_in, bwd_in, bwd_in, bwd_in,
                  pl.BlockSpec(masks.shape, lambda b, j: (0, 0, 0, 0))],
        out_specs=[fwd, bwd],
        out_shape=[hm_shape, hm_shape],
        scratch_shapes=[pltpu.VMEM((2, H, C_VAL_DIM, C_KEY_DIM), F32),
                        pltpu.VMEM((N_SCAN_OPERANDS, T * HEAD_GROUP, LANES), F32),
                        pltpu.VMEM((N_SCAN_OPERANDS, T * HEAD_GROUP, LANES), F32),
                        pltpu.VMEM((SUBLANES, LANES), F32),
                        pltpu.VMEM((SUBLANES, LANES), F32)],
        compiler_params=_params(2),
        name="c_scan",
    )(q, ff, v, q, fb, v, masks)


def _c_out_kernel(of_ref, ob_ref, sg_ref, x_ref, mod_ref, gn_ref, w_ref, g_ref, b_ref, o_ref):
    nb, tm = x_ref.shape[0], x_ref.shape[1]
    rows = []
    for bb in range(nb):
        cols = []
        for hd in range(C_HEADS):
            o = of_ref[bb, hd].astype(F32) + ob_ref[bb, hd].astype(F32)
            ms = jnp.mean(o * o, axis=-1, keepdims=True)
            cols.append(o * lax.rsqrt(ms + RMS_EPS) * gn_ref[...])
        rows.append(jnp.concatenate(cols, axis=1) * sg_ref[bb])
    y = _dot(jnp.concatenate(rows, axis=0).astype(BF16), w_ref[...])
    for bb in range(nb):
        gate = mod_ref[bb, 0, 2:3, :]
        o_ref[bb] = _layer_norm(DEEPNORM_ALPHA * x_ref[bb] + gate * y[bb * tm:(bb + 1) * tm], g_ref[...], b_ref[...])


def _c_out(xcat, modsel, o_f, o_b, sg, gnorm, w_out, ln_g, ln_b, ctx_len, latent_only):
    B, L, _ = xcat.shape
    tm = ROW_TILE
    nb = BATCH_PAIR if B % BATCH_PAIR == 0 else 1
    ctx_tiles = ctx_len // tm
    skip = ctx_tiles if latent_only else 0
    n_tiles = L // tm - skip
    hm = pl.BlockSpec((nb, C_HEADS, tm, LANES), lambda b, i: (b, 0, i + skip, 0))
    tok = pl.BlockSpec((nb, tm, D_MODEL), lambda b, i: (b, i + skip, 0))
    row = pl.BlockSpec((1, D_MODEL), lambda b, i: (0, 0))
    return pl.pallas_call(
        _c_out_kernel,
        grid=(B // nb, n_tiles),
        in_specs=[
            hm, hm, tok, tok,
            pl.BlockSpec((nb, 1, 3, D_MODEL), lambda b, i: (b, (i + skip >= ctx_tiles).astype(jnp.int32), 0, 0)),
            pl.BlockSpec((1, LANES), lambda b, i: (0, 0)),
            pl.BlockSpec((D_MODEL, D_MODEL), lambda b, i: (0, 0)),
            row, row,
        ],
        out_specs=pl.BlockSpec((nb, tm, D_MODEL), lambda b, i: (b, i, 0)),
        out_shape=jax.ShapeDtypeStruct((B, n_tiles * tm, D_MODEL), F32),
        compiler_params=_params(2),
        name="c_out",
    )(o_f, o_b, sg, xcat, modsel, gnorm, w_out, ln_g, ln_b)


def _rope_tables(seq, ctx_len):
    t = jnp.arange(seq)
    freqs = ROPE_BASE ** (-jnp.arange(16, dtype=F32) / 16)
    ang_r = (t // GRID_W).astype(F32)[:, None] * freqs[None, :]
    ang_c = (t % GRID_W).astype(F32)[:, None] * freqs[None, :]
    cos = jnp.concatenate([jnp.cos(ang_r)] * 2 + [jnp.cos(ang_c)] * 2, axis=-1)
    sin = jnp.concatenate([-jnp.sin(ang_r), jnp.sin(ang_r), -jnp.sin(ang_c), jnp.sin(ang_c)], axis=-1)
    cos = jnp.concatenate([jnp.ones((ctx_len, 64), F32), cos], axis=0)
    sin = jnp.concatenate([jnp.zeros((ctx_len, 64), F32), sin], axis=0)
    return jnp.tile(cos, (1, 2)), jnp.tile(sin, (1, 2))


def kernel(x, c, ctx, c_ctx, w_ada, b_ada, ln_g, ln_b, w_in_ab, w_out_ab, sink_ab, conv_ab,
           w_in_c, w_out_c, lb_c, gnorm_c):
    B, seq, _ = x.shape
    ctx_len = ctx.shape[1]
    depth = w_ada.shape[0]
    assert seq % ROW_TILE == 0 and ctx_len % ROW_TILE == 0 and B <= 8

    s = jnp.concatenate([c, c_ctx[None, :], jnp.zeros((16 - B - 1, D_MODEL), F32)], axis=0)
    mod_all = _ada(s, w_ada, b_ada)
    mod_lat = mod_all[:, :B].reshape(depth, B, 1, 3, D_MODEL)
    mod_ctx = jnp.broadcast_to(mod_all[:, B].reshape(depth, 1, 1, 3, D_MODEL), mod_lat.shape)
    modsel = jnp.concatenate([mod_ctx, mod_lat], axis=2)

    lb_p = jax.nn.softmax(lb_c.astype(F32), axis=1)
    lb_all = jnp.cumsum(lb_p, axis=1) - lb_p[:, :1]

    cos_t, sin_t = _rope_tables(seq, ctx_len)
    xcat = None
    for l in range(depth):
        j = l // 2
        g_row, b_row = ln_g[l][None, :], ln_b[l][None, :]
        if l % 2 == 0:
            stream = (ctx, x, ctx_len) if xcat is None else (xcat, xcat, 0)
            q, k, v, sga, obg = _ab_in(stream, modsel[l], w_in_ab[j].astype(BF16), conv_ab[j], cos_t, sin_t, ctx_len)
            sink_b = jnp.broadcast_to(sink_ab[j].astype(F32)[:, None], (A_HEADS, LANES))
            xcat = _ab_attn(stream, modsel[l], q, k, v, sga, obg, sink_b, w_out_ab[j].astype(BF16),
                            g_row, b_row, ctx_len)
        else:
            q, ff, fb, v, sg = _c_in(xcat, modsel[l], w_in_c[j].astype(BF16), lb_all[:, j], ctx_len)
            o_f, o_b = _c_scan(q, ff, fb, v, ctx_len)
            xcat = _c_out(xcat, modsel[l], o_f, o_b, sg, gnorm_c[j][None, :], w_out_c[j].astype(BF16),
                          g_row, b_row, ctx_len, latent_only=(l == depth - 1))
    return xcat if depth % 2 == 0 else xcat[:, ctx_len:]
```

```python
import functools

import numpy as np
import jax
import jax.numpy as jnp
from jax import lax
from jax.experimental import pallas as pl
from jax.experimental.pallas import tpu as pltpu

F32 = jnp.float32
BF16 = jnp.bfloat16

D_MODEL = 1024
DEPTH = 4
GRID_W = 64
A_HEADS = 8
A_KV_HEADS = 2
A_HEAD_DIM = 64
A_WIDTH = A_HEADS * A_HEAD_DIM
A_KV_WIDTH = A_KV_HEADS * A_HEAD_DIM
WINDOW = 128
ROPE_BASE = 10000.0
B_WIDTH = D_MODEL // 2
C_HEADS = 8
C_KEY_DIM = 128
C_VAL_DIM = D_MODEL // C_HEADS
C_WIDTH = C_HEADS * C_KEY_DIM
DEEPNORM_ALPHA = (2 * DEPTH) ** 0.25
LN_EPS = 1e-5
RMS_EPS = 1e-6
LOG2_E = 1.4426950408889634

LANES = 128
SUBLANES = 8
ROW_TILE = 256
SCAN_TILE = 128
BATCH_PAIR = 2
SCAN_LEVELS = (1, 2, 4, 8, 16, 32, 64)
HEAD_GROUP = 4
VMEM_LIMIT = 56 * 1024 * 1024


def _params(n_axes):
    return pltpu.CompilerParams(dimension_semantics=("arbitrary",) * n_axes, vmem_limit_bytes=VMEM_LIMIT)


def _dot(a, b):
    return jnp.dot(a, b, preferred_element_type=F32)


def _dot_nt(a, b):
    return lax.dot_general(a, b, (((1,), (1,)), ((), ())), preferred_element_type=F32)


def _silu(t):
    return t * jax.nn.sigmoid(t)


def _modulate(x, mod_ref, bb=0):
    shift = mod_ref[bb, 0, 0:1, :]
    scale = mod_ref[bb, 0, 1:2, :]
    return x * (1 + scale) + shift


def _layer_norm(r, g, b):
    mu = jnp.mean(r, axis=-1, keepdims=True)
    rc = r - mu
    var = jnp.mean(rc * rc, axis=-1, keepdims=True)
    return rc * lax.rsqrt(var + LN_EPS) * g + b


def _ada_kernel(s_ref, w_ref, b_ref, o_ref):
    s = _silu(s_ref[...]).astype(BF16)
    o_ref[0] = _dot(s, w_ref[0].astype(BF16)) + b_ref[0]


def _ada(s, w_ada, b_ada):
    depth = w_ada.shape[0]
    rows = s.shape[0]
    n_col = 3 * D_MODEL // D_MODEL
    return pl.pallas_call(
        _ada_kernel,
        grid=(depth, n_col),
        in_specs=[
            pl.BlockSpec((rows, D_MODEL), lambda l, n: (0, 0)),
            pl.BlockSpec((1, D_MODEL, D_MODEL), lambda l, n: (l, 0, n)),
            pl.BlockSpec((1, 1, D_MODEL), lambda l, n: (l, 0, n)),
        ],
        out_specs=pl.BlockSpec((1, rows, D_MODEL), lambda l, n: (l, 0, n)),
        out_shape=jax.ShapeDtypeStruct((depth, rows, 3 * D_MODEL), F32),
        compiler_params=_params(2),
        name="ada_mod",
    )(s, w_ada, b_ada.reshape(depth, 1, 3 * D_MODEL))


def _rope(t, cos, sin_signed, lane_lo):
    partner = jnp.where(lane_lo, pltpu.roll(t, LANES - 16, axis=1), pltpu.roll(t, 16, axis=1))
    return t * cos + partner * sin_signed


def _ab_in_kernel(xc_ref, xl_ref, xpc_ref, xpl_ref, xnc_ref, xnl_ref, mod_ref, w_ref, cw_ref, cos_ref, sin_ref,
                  q_ref, k_ref, v_ref, sga_ref, obg_ref, *, n_tiles, ctx_tiles):
    i = pl.program_id(1)
    tm = xc_ref.shape[1]
    is_ctx = i < ctx_tiles
    h = _modulate(jnp.where(is_ctx, xc_ref[0], xl_ref[0]), mod_ref).astype(BF16)

    def proj(lhs, lo, hi):
        return _dot(lhs, w_ref[:, lo:hi])

    o_q, o_k, o_v, o_ga = 0, A_WIDTH, A_WIDTH + A_KV_WIDTH, A_WIDTH + 2 * A_KV_WIDTH
    o_xb = o_ga + A_WIDTH
    o_bg, o_cg, o_gb = o_xb + B_WIDTH, o_xb + 2 * B_WIDTH, o_xb + 3 * B_WIDTH

    cos = cos_ref[...]
    sin_signed = sin_ref[...]
    lane = lax.broadcasted_iota(jnp.int32, (tm, LANES), 1)
    lane_lo = (lane % 32) < 16
    q = proj(h, o_q, o_k)
    for j in range(A_WIDTH // LANES):
        qj = _rope(q[:, j * LANES:(j + 1) * LANES], cos, sin_signed, lane_lo)
        q_ref[0, :, j * LANES:(j + 1) * LANES] = qj * (A_HEAD_DIM ** -0.5 * LOG2_E)
    k_ref[0] = _rope(proj(h, o_k, o_v), cos, sin_signed, lane_lo)
    v_ref[0] = proj(h, o_v, o_ga)
    sga_ref[0] = _silu(proj(h, o_ga, o_xb))

    u = proj(h, o_cg, o_gb) * proj(h, o_xb, o_bg)
    halo = jnp.concatenate([jnp.where(is_ctx, xpc_ref[0], xpl_ref[0]), jnp.where(is_ctx, xnc_ref[0], xnl_ref[0])],
                           axis=0)
    hh = _modulate(halo, mod_ref).astype(BF16)
    uh = proj(hh, o_cg, o_gb) * proj(hh, o_xb, o_bg)
    left_ok = jnp.logical_and(i != 0, i != ctx_tiles)
    right_ok = jnp.logical_and(i != ctx_tiles - 1, i != n_tiles - 1)
    u_left = jnp.where(left_ok, uh[SUBLANES - 1:SUBLANES], 0.0)
    u_right = jnp.where(right_ok, uh[SUBLANES:SUBLANES + 1], 0.0)
    row = lax.broadcasted_iota(jnp.int32, u.shape, 0)
    u_prev = jnp.where(row == 0, u_left, pltpu.roll(u, 1, axis=0))
    u_next = jnp.where(row == tm - 1, u_right, pltpu.roll(u, tm - 1, axis=0))
    conv = u_prev * cw_ref[0:1, :] + u * cw_ref[1:2, :] + u_next * cw_ref[2:3, :]
    obg_ref[0] = proj(h, o_bg, o_cg) * conv * _silu(proj(h, o_gb, o_gb + B_WIDTH))


def _stream_len(stream):
    return stream[1].shape[1] + stream[2]


def _stream_windows(stream, rows, ctx_len, index):
    ctx_arr, lat_arr, shift = stream
    ctx_blocks, n_blocks, skip = ctx_len // rows, _stream_len(stream) // rows, shift // rows
    specs = [
        pl.BlockSpec((1, rows, D_MODEL), lambda b, i: (b, jnp.clip(index(i), 0, ctx_blocks - 1), 0)),
        pl.BlockSpec((1, rows, D_MODEL), lambda b, i: (b, jnp.clip(index(i), ctx_blocks, n_blocks - 1) - skip, 0)),
    ]
    return specs, [ctx_arr, lat_arr]


def _ab_in(stream, modsel, w_in, conv_w, cos_t, sin_t, ctx_len):
    B, L = stream[0].shape[0], _stream_len(stream)
    tm = ROW_TILE
    n_tiles = L // tm
    ctx_tiles = ctx_len // tm
    halo_blocks = tm // SUBLANES
    kern = functools.partial(_ab_in_kernel, n_tiles=n_tiles, ctx_tiles=ctx_tiles)
    width = w_in.shape[1]
    tok = lambda w: pl.BlockSpec((1, tm, w), lambda b, i: (b, i, 0))
    x_specs, x_args = [], []
    for rows, index in ((tm, lambda i: i), (SUBLANES, lambda i: i * halo_blocks - 1),
                        (SUBLANES, lambda i: (i + 1) * halo_blocks)):
        specs, args = _stream_windows(stream, rows, ctx_len, index)
        x_specs += specs
        x_args += args
    return pl.pallas_call(
        kern,
        grid=(B, n_tiles),
        in_specs=x_specs + [
            pl.BlockSpec((1, 1, 3, D_MODEL), lambda b, i: (b, (i >= ctx_tiles).astype(jnp.int32), 0, 0)),
            pl.BlockSpec((D_MODEL, width), lambda b, i: (0, 0)),
            pl.BlockSpec((3, B_WIDTH), lambda b, i: (0, 0)),
            pl.BlockSpec((tm, LANES), lambda b, i: (i, 0)),
            pl.BlockSpec((tm, LANES), lambda b, i: (i, 0)),
        ],
        out_specs=[tok(A_WIDTH), tok(A_KV_WIDTH), tok(A_KV_WIDTH), tok(A_WIDTH), tok(B_WIDTH)],
        out_shape=[
            jax.ShapeDtypeStruct((B, L, A_WIDTH), F32),
            jax.ShapeDtypeStruct((B, L, A_KV_WIDTH), F32),
            jax.ShapeDtypeStruct((B, L, A_KV_WIDTH), F32),
            jax.ShapeDtypeStruct((B, L, A_WIDTH), F32),
            jax.ShapeDtypeStruct((B, L, B_WIDTH), F32),
        ],
        compiler_params=_params(2),
        name="ab_in",
    )(*x_args, modsel, w_in, conv_w, cos_t, sin_t)


def _ab_attn_kernel(q_ref, kx_ref, kp_ref, kc_ref, kn_ref, vx_ref, vp_ref, vc_ref, vn_ref,
                    sga_ref, obg_ref, xc_ref, xl_ref, mod_ref, sink_ref, w_ref, g_ref, b_ref, o_ref,
                    *, n_blk, ctx_blk):
    i = pl.program_id(1)
    W = WINDOW
    n_sub = q_ref.shape[1] // W
    lc = kx_ref.shape[1]
    half = LANES // 2
    group = A_HEADS // A_KV_HEADS

    k_ctx, v_ctx = kx_ref[0].astype(BF16), vx_ref[0].astype(BF16)
    k_loc = jnp.concatenate([kp_ref[0], kc_ref[0], kn_ref[0]], axis=0).astype(BF16)
    v_loc = jnp.concatenate([vp_ref[0], vc_ref[0], vn_ref[0]], axis=0).astype(BF16)

    is_lat = i * n_sub >= ctx_blk
    ri = lax.broadcasted_iota(jnp.int32, (W, W), 0)
    ci = lax.broadcasted_iota(jnp.int32, (W, W), 1)
    neg = jnp.float32(-jnp.inf)
    lane = lax.broadcasted_iota(jnp.int32, (W, LANES), 1)
    low = lane < half

    def logits(sb, kvh):
        blk = i * n_sub + sb
        prev_ok = jnp.logical_and(is_lat, blk - 1 >= ctx_blk)
        next_ok = jnp.logical_and(is_lat, blk + 1 <= n_blk - 1)
        bias = jnp.concatenate([
            jnp.where(jnp.logical_and(prev_ok, ci >= ri), 0.0, neg),
            jnp.where(jnp.logical_and(is_lat, ci >= 0), 0.0, neg),
            jnp.where(jnp.logical_and(next_ok, ci <= ri), 0.0, neg),
        ], axis=1)
        bias = jnp.concatenate([bias] * group, axis=0)
        on_half = low if kvh == 0 else jnp.logical_not(low)
        stack, sinks = [], []
        for g in range(group):
            hd = kvh * group + g
            qh = q_ref[0, sb * W:(sb + 1) * W, (hd // 2) * LANES:(hd // 2 + 1) * LANES]
            if hd % 2 != kvh:
                qh = pltpu.roll(qh, half, axis=1)
            stack.append(jnp.where(on_half, qh, 0.0))
            sinks.append(jnp.broadcast_to(sink_ref[hd:hd + 1, :] * LOG2_E, (W, LANES)))
        k_sb = jnp.concatenate([k_ctx, k_loc[sb * W:(sb + 3) * W]], axis=0)
        s = _dot_nt(jnp.concatenate(stack, axis=0).astype(BF16), k_sb)
        return jnp.concatenate([s[:, :lc], s[:, lc:] + bias], axis=1), jnp.concatenate(sinks, axis=0)

    def attend(sb, s, sk):
        v_sb = jnp.concatenate([v_ctx, v_loc[sb * W:(sb + 3) * W]], axis=0)
        m = jnp.maximum(jnp.max(s, axis=-1, keepdims=True), sk)
        e = jnp.exp2(s - jnp.concatenate([m] * (s.shape[1] // LANES), axis=1))
        denom = jnp.sum(e, axis=-1, keepdims=True) + jnp.exp2(sk - m)
        return _dot(e.astype(BF16), v_sb) / denom

    stacks = [(sb, kvh) for sb in range(n_sub) for kvh in range(A_KV_HEADS)]
    pieces = {}
    nxt = logits(*stacks[0])
    for idx, (sb, kvh) in enumerate(stacks):
        cur = nxt
        if idx + 1 < len(stacks):
            nxt = logits(*stacks[idx + 1])
        pv = attend(sb, *cur)
        for g in range(group):
            hd = kvh * group + g
            piece = pv[g * W:(g + 1) * W]
            pieces[sb, hd] = piece if hd % 2 == kvh else pltpu.roll(piece, half, axis=1)
    o_a = jnp.concatenate([
        jnp.concatenate([jnp.where(low, pieces[sb, 2 * j], pieces[sb, 2 * j + 1]) for j in range(A_HEADS // 2)], axis=1)
        for sb in range(n_sub)], axis=0)

    y_in = jnp.concatenate([o_a * sga_ref[0], obg_ref[0]], axis=1).astype(BF16)
    y = _dot(y_in, w_ref[...])
    gate = mod_ref[0, 0, 2:3, :]
    x = jnp.where(is_lat, xl_ref[0], xc_ref[0])
    o_ref[0] = _layer_norm(DEEPNORM_ALPHA * x + gate * y, g_ref[...], b_ref[...])


def _ab_attn(stream, modsel, q, k, v, sga, obg, sink_b, w_out, ln_g, ln_b, ctx_len):
    B, L = stream[0].shape[0], _stream_len(stream)
    T = ROW_TILE
    W = WINDOW
    n_sub = T // W
    n_blk = L // W
    ctx_blk = ctx_len // W
    kern = functools.partial(_ab_attn_kernel, n_blk=n_blk, ctx_blk=ctx_blk)
    tok = lambda w: pl.BlockSpec((1, T, w), lambda b, i: (b, i, 0))
    prev = pl.BlockSpec((1, W, A_KV_WIDTH), lambda b, i: (b, jnp.maximum(i * n_sub - 1, 0), 0))
    nxt = pl.BlockSpec((1, W, A_KV_WIDTH), lambda b, i: (b, jnp.minimum((i + 1) * n_sub, n_blk - 1), 0))
    ctxs = pl.BlockSpec((1, ctx_len, A_KV_WIDTH), lambda b, i: (b, 0, 0))
    row = pl.BlockSpec((1, D_MODEL), lambda b, i: (0, 0))
    x_specs, x_args = _stream_windows(stream, T, ctx_len, lambda i: i)
    return pl.pallas_call(
        kern,
        grid=(B, L // T),
        in_specs=[
            tok(A_WIDTH),
            ctxs, prev, tok(A_KV_WIDTH), nxt,
            ctxs, prev, tok(A_KV_WIDTH), nxt,
            tok(A_WIDTH), tok(B_WIDTH), *x_specs,
            pl.BlockSpec((1, 1, 3, D_MODEL), lambda b, i: (b, (i * n_sub >= ctx_blk).astype(jnp.int32), 0, 0)),
            pl.BlockSpec((A_HEADS, LANES), lambda b, i: (0, 0)),
            pl.BlockSpec((D_MODEL, D_MODEL), lambda b, i: (0, 0)),
            row, row,
        ],
        out_specs=tok(D_MODEL),
        out_shape=jax.ShapeDtypeStruct((B, L, D_MODEL), F32),
        compiler_params=_params(2),
        name="ab_attn_out",
    )(q, k, k, k, k, v, v, v, v, sga, obg, *x_args, modsel, sink_b, w_out, ln_g, ln_b)


def _forget_gate(z, lb):
    e = jnp.exp(-jnp.abs(z))
    return jnp.where(z >= 0, 1.0 + lb * e, lb + e) / (1.0 + e)


def _c_in_kernel(x_ref, mod_ref, w_ref, lb_ref, q_ref, ff_ref, fb_ref, v_ref, sg_ref):
    nb, tm = x_ref.shape[0], x_ref.shape[1]
    h = jnp.concatenate([_modulate(x_ref[bb], mod_ref, bb) for bb in range(nb)], axis=0).astype(BF16)

    def proj(idx):
        return _dot(h, w_ref[:, idx * C_WIDTH:(idx + 1) * C_WIDTH])

    def put(ref, val):
        for bb in range(nb):
            for hd in range(C_HEADS):
                g, hh = divmod(hd, HEAD_GROUP)
                ref[bb, g, pl.ds(hh, tm, stride=HEAD_GROUP), :] = val[bb * tm:(bb + 1) * tm, hd * LANES:(hd + 1) * LANES]

    for d, f_ref in enumerate((ff_ref, fb_ref)):
        put(f_ref, _forget_gate(proj(1 + d), jnp.clip(lb_ref[d:d + 1, :], 0.0, 1.0)))
    put(q_ref, _silu(proj(0)) * (C_KEY_DIM ** -0.5))
    sg_ref[...] = _silu(proj(4)).astype(sg_ref.dtype).reshape(sg_ref.shape)
    put(v_ref, proj(3))


def _c_in(xcat, modsel, w_in, lb, ctx_len):
    B, L, _ = xcat.shape
    tm = ROW_TILE
    nb = BATCH_PAIR if B % BATCH_PAIR == 0 else 1
    n_tiles = L // tm
    ctx_tiles = ctx_len // tm
    width = w_in.shape[1]
    groups = C_HEADS // HEAD_GROUP
    hm = pl.BlockSpec((nb, groups, tm * HEAD_GROUP, LANES), lambda b, i: (b, 0, i, 0))
    hm_shape = jax.ShapeDtypeStruct((B, groups, L * HEAD_GROUP, LANES), F32)
    return pl.pallas_call(
        _c_in_kernel,
        grid=(B // nb, n_tiles),
        in_specs=[
            pl.BlockSpec((nb, tm, D_MODEL), lambda b, i: (b, i, 0)),
            pl.BlockSpec((nb, 1, 3, D_MODEL), lambda b, i: (b, (i >= ctx_tiles).astype(jnp.int32), 0, 0)),
            pl.BlockSpec((D_MODEL, width), lambda b, i: (0, 0)),
            pl.BlockSpec((2, C_WIDTH), lambda b, i: (0, 0)),
        ],
        out_specs=[hm, hm, hm, hm, pl.BlockSpec((nb, tm, C_WIDTH), lambda b, i: (b, i, 0))],
        out_shape=[hm_shape] * 4 + [jax.ShapeDtypeStruct((B, L, C_WIDTH), BF16)],
        compiler_params=_params(2),
        name="c_in",
    )(xcat, modsel, w_in, lb)


def _query_blocks(n, m, reverse):
    return [p for p in range(0, n, m) if ((p // m) % 2 == 1) != reverse]


SCAN_CHUNK = 16
N_LOW_LEVELS = sum(m < SCAN_CHUNK for m in SCAN_LEVELS)
N_SCAN_OPERANDS = N_LOW_LEVELS + 3
N_SCAN_FACTORS = len(SCAN_LEVELS) - N_LOW_LEVELS + 2
LOOKAHEAD = 2


def _scan_elementwise(q_ref, f_ref, v_ref, reverse, out_ref, factor_ref, decay_ref):
    assert SUBLANES == 2 * HEAD_GROUP
    n_rows, width = q_ref.shape
    chunk_rows = SCAN_CHUNK * HEAD_GROUP
    grouped = lambda t: t.reshape(t.shape[0] // SUBLANES, SUBLANES, width)
    low1 = lax.broadcasted_iota(jnp.int32, (1, SUBLANES, width), 1) < HEAD_GROUP
    swap = lambda t: pltpu.roll(t, HEAD_GROUP, axis=1)

    def block_total(t):
        return jnp.where(low1, t, swap(t)) if reverse else jnp.where(low1, swap(t), t)

    def double(q, k, run, rest, g):
        z, new_run, new_rest = [], [], []
        for a in range(0, run.shape[0], 2 * g):
            lo, hi = slice(a, a + g), slice(a + g, a + 2 * g)
            first, last = (hi, lo) if reverse else (lo, hi)
            edge = (lambda blk: run[blk.start:blk.start + 1]) if reverse else (lambda blk: run[blk.stop - 1:blk.stop])
            parts = {
                first: (k[first] * rest[first], run[first], rest[first] * block_total(edge(last))),
                last: (q[last] * run[last], run[last] * block_total(edge(first)), rest[last]),
            }
            for blk in (lo, hi):
                z.append(parts[blk][0])
                new_run.append(parts[blk][1])
                new_rest.append(parts[blk][2])
        return tuple(jnp.concatenate(t, axis=0) for t in (z, new_run, new_rest))

    def put(idx, rows, t):
        out_ref[idx, rows, :] = t.reshape(t.shape[0] * SUBLANES, width)

    low_levels = [m for m in SCAN_LEVELS if m < SCAN_CHUNK]
    n_low, n_lvl = len(low_levels), len(SCAN_LEVELS)
    chunks = [slice(r0, r0 + chunk_rows) for r0 in range(0, n_rows, chunk_rows)]
    total = []
    for rows in chunks:
        q, f, v = grouped(q_ref[rows, :]), grouped(f_ref[rows, :]), grouped(v_ref[rows, :])
        k = 1.0 - f
        put(n_low + 2, rows, jnp.sum(q * k, axis=-1, keepdims=True) * v)
        second = low1 if reverse else jnp.logical_not(low1)
        other = swap(f)
        put(0, rows, jnp.where(second, q * f, k))
        run = jnp.where(second, f * other, f)
        rest = jnp.where(second, 1.0, other)
        for lvl, m in enumerate(low_levels[1:], start=1):
            z, run, rest = double(q, k, run, rest, m // 2)
            put(lvl, rows, z)
        put(n_low, rows, q * run)
        put(n_low + 1, rows, k * rest)
        total.append(block_total(run[0:1] if reverse else run[run.shape[0] - 1:]))

    mul = lambda a, b: b if a is None else a * b
    ones = jnp.ones((SUBLANES, width), F32)
    keep = lambda u, c, t: factor_ref.__setitem__((u, c), ones if t is None else t[0])
    before, after = [None] * len(chunks), [None] * len(chunks)
    for u, m in enumerate(SCAN_LEVELS[n_low:]):
        g = m // SCAN_CHUNK
        new_total = []
        for a in range(0, len(chunks), 2 * g):
            lo, hi = list(range(a, a + g)), list(range(a + g, a + 2 * g))
            first, last = (hi, lo) if reverse else (lo, hi)
            t_first, t_last = total[first[0] // g], total[last[0] // g]
            for c in first:
                keep(u, c, after[c])
                after[c] = mul(after[c], t_last)
            for c in last:
                keep(u, c, before[c])
                before[c] = mul(before[c], t_first)
            new_total.append(t_first * t_last)
        total = new_total
    n_up = n_lvl - n_low
    for c in range(len(chunks)):
        keep(n_up, c, before[c])
        keep(n_up + 1, c, after[c])
    decay_ref[...] = total[0][0]


def _scan_products(load, factor, v, decay, state_t, mask_ref, d):
    n = v.shape[0]
    reverse = d == 1
    n_lvl = len(SCAN_LEVELS)
    blocks = [None] * (n // SUBLANES)

    def add_rows(start, term):
        for r in range(0, term.shape[0], SUBLANES):
            b = (start + r) // SUBLANES
            piece = term[r:r + SUBLANES]
            blocks[b] = piece if blocks[b] is None else blocks[b] + piece

    n_low = N_LOW_LEVELS
    q_run, k_rest = load(n_low), load(n_low + 1)
    chunk = lambda t, c: t[c * SCAN_CHUNK:(c + 1) * SCAN_CHUNK]
    n_chunks = n // SCAN_CHUNK

    def scaled(u, pick):
        return [chunk(q_run if pick(c) else k_rest, c) * factor(u, c) for c in range(n_chunks)]

    for lvl, m in enumerate(SCAN_LEVELS):
        if m < SUBLANES:
            zb = load(lvl).astype(BF16)
            add_rows(0, _dot_nt(zb, zb) * mask_ref[d, lvl])
            continue
        starts = _query_blocks(n, m, reverse)
        if lvl < n_low:
            z = load(lvl)
        else:
            on_query_side = lambda c: any(p <= c * SCAN_CHUNK < p + m for p in starts)
            z = jnp.concatenate(scaled(lvl - n_low, on_query_side), axis=0)
        zq = jnp.concatenate([z[p:p + m] for p in starts], axis=0).astype(BF16)
        mask = jnp.concatenate([mask_ref[d, lvl, p:p + m, :] for p in starts], axis=0)
        term = _dot_nt(zq, z.astype(BF16)) * mask
        for idx, p in enumerate(starts):
            add_rows(p, term[idx * m:(idx + 1) * m])
    weights = jnp.concatenate(blocks, axis=0).astype(BF16)
    q_in = jnp.concatenate(scaled(n_lvl - n_low, lambda c: True), axis=0).astype(BF16)
    k_out = jnp.concatenate(scaled(n_lvl - n_low + 1, lambda c: False), axis=0).astype(BF16)
    own = load(n_low + 2)
    o_partial = _dot_nt(q_in, state_t.astype(BF16)) + own
    new_state = state_t * decay + _dot(v.T.astype(BF16), k_out)
    return weights, v.astype(BF16), o_partial, new_state


def _c_scan_kernel(qf_ref, ff_ref, vf_ref, qb_ref, fb_ref, vb_ref, mask_ref, of_ref, ob_ref,
                   st_ref, op0_ref, op1_ref, fac0_ref, fac1_ref, decay0_ref, decay1_ref):
    @pl.when(pl.program_id(1) == 0)
    def _():
        st_ref[...] = jnp.zeros(st_ref.shape, F32)

    T = SCAN_TILE
    n_tiles = of_ref.shape[2] // T
    dirs = ((qf_ref, ff_ref, vf_ref, of_ref), (qb_ref, fb_ref, vb_ref, ob_ref))
    units = [(n_tiles - 1 - t if d == 1 else t, g, d)
             for t in range(n_tiles) for g in range(C_HEADS // HEAD_GROUP) for d in range(2)]
    op_refs, fac_refs, decay_refs = (op0_ref, op1_ref), (fac0_ref, fac1_ref), (decay0_ref, decay1_ref)

    def elementwise(idx):
        tile, g, d = units[idx]
        q_r, f_r, v_r, _ = dirs[d]
        slot = idx % 2
        rows = pl.ds(tile * T * HEAD_GROUP, T * HEAD_GROUP)
        _scan_elementwise(q_r.at[0, g, rows], f_r.at[0, g, rows], v_r.at[0, g, rows], d == 1,
                          op_refs[slot], fac_refs[slot], decay_refs[slot])

    def finish(job):
        (tile, d, hd), (weights, v_bf, o_partial, _) = job
        o_r = dirs[d][3]
        o_r[0, hd, tile * T:(tile + 1) * T, :] = (o_partial + _dot(weights, v_bf)).astype(o_r.dtype)

    elementwise(0)
    pending = []
    for idx, (tile, g, d) in enumerate(units):
        if idx + 1 < len(units):
            elementwise(idx + 1)
        slot = idx % 2
        for hh in range(HEAD_GROUP):
            hd = g * HEAD_GROUP + hh
            rows = pl.ds(hh, T, stride=HEAD_GROUP)
            load = lambda i, rows=rows, slot=slot: op_refs[slot][i, rows, :]
            v_rows = pl.ds(tile * T * HEAD_GROUP + hh, T, stride=HEAD_GROUP)
            factor = lambda u, c, hh=hh, slot=slot: fac_refs[slot][u, c, hh:hh + 1, :]
            job = ((tile, d, hd), _scan_products(load, factor, dirs[d][2][0, g, v_rows, :],
                                                 decay_refs[slot][hh:hh + 1, :], st_ref[d, hd], mask_ref, d))
            st_ref[d, hd] = job[1][3]
            pending.append(job)
            if len(pending) > LOOKAHEAD:
                finish(pending.pop(0))
    for job in pending:
        finish(job)


def _scan_masks():
    n = SCAN_TILE
    c = np.arange(n)[:, None]
    s = np.arange(n)[None, :]
    fwd = []
    for m in SCAN_LEVELS:
        fwd.append(((c // (2 * m) == s // (2 * m)) & (c % (2 * m) >= m) & (s % (2 * m) < m)).astype(np.float32))
    fwd = np.stack(fwd)
    return jnp.asarray(np.stack([fwd, fwd.transpose(0, 2, 1)]))


def _c_scan(q, ff, fb, v, ctx_len):
    B, groups, rows, _ = q.shape
    H = groups * HEAD_GROUP
    L = rows // HEAD_GROUP
    T = SCAN_TILE
    step = ROW_TILE
    assert step % T == 0 and L % step == 0 and ctx_len % step == 0
    n = L // step
    nc = ctx_len // step
    masks = _scan_masks()
    back = lambda j: jnp.where(j < nc, nc - 1 - j, n - 1 - (j - nc))
    fwd_in = pl.BlockSpec((1, groups, step * HEAD_GROUP, LANES), lambda b, j: (b, 0, j, 0))
    bwd_in = pl.BlockSpec((1, groups, step * HEAD_GROUP, LANES), lambda b, j: (b, 0, back(j), 0))
    fwd = pl.BlockSpec((1, H, step, LANES), lambda b, j: (b, 0, j, 0))
    bwd = pl.BlockSpec((1, H, step, LANES), lambda b, j: (b, 0, back(j), 0))
    hm_shape = jax.ShapeDtypeStruct((B, H, L, LANES), BF16)
    return pl.pallas_call(
        _c_scan_kernel,
        grid=(B, n),
        in_specs=[fwd_in, fwd_in, fwd_in, bwd_in, bwd_in, bwd_in,
                  pl.BlockSpec(masks.shape, lambda b, j: (0, 0, 0, 0))],
        out_specs=[fwd, bwd],
        out_shape=[hm_shape, hm_shape],
        scratch_shapes=[pltpu.VMEM((2, H, C_VAL_DIM, C_KEY_DIM), F32),
                        pltpu.VMEM((N_SCAN_OPERANDS, T * HEAD_GROUP, LANES), F32),
                        pltpu.VMEM((N_SCAN_OPERANDS, T * HEAD_GROUP, LANES), F32),
                        pltpu.VMEM((N_SCAN_FACTORS, T // SCAN_CHUNK, SUBLANES, LANES), F32),
                        pltpu.VMEM((N_SCAN_FACTORS, T // SCAN_CHUNK, SUBLANES, LANES), F32),
                        pltpu.VMEM((SUBLANES, LANES), F32),
                        pltpu.VMEM((SUBLANES, LANES), F32)],
        compiler_params=_params(2),
        name="c_scan",
    )(q, ff, v, q, fb, v, masks)


def _c_out_kernel(of_ref, ob_ref, sg_ref, x_ref, mod_ref, gn_ref, w_ref, g_ref, b_ref, o_ref):
    nb, tm = x_ref.shape[0], x_ref.shape[1]
    rows = []
    for bb in range(nb):
        cols = []
        for hd in range(C_HEADS):
            o = of_ref[bb, hd].astype(F32) + ob_ref[bb, hd].astype(F32)
            ms = jnp.mean(o * o, axis=-1, keepdims=True)
            cols.append(o * lax.rsqrt(ms + RMS_EPS) * gn_ref[...])
        rows.append(jnp.concatenate(cols, axis=1) * sg_ref[bb])
    y = _dot(jnp.concatenate(rows, axis=0).astype(BF16), w_ref[...])
    for bb in range(nb):
        gate = mod_ref[bb, 0, 2:3, :]
        o_ref[bb] = _layer_norm(DEEPNORM_ALPHA * x_ref[bb] + gate * y[bb * tm:(bb + 1) * tm], g_ref[...], b_ref[...])


def _c_out(xcat, modsel, o_f, o_b, sg, gnorm, w_out, ln_g, ln_b, ctx_len, latent_only):
    B, L, _ = xcat.shape
    tm = ROW_TILE
    nb = BATCH_PAIR if B % BATCH_PAIR == 0 else 1
    ctx_tiles = ctx_len // tm
    skip = ctx_tiles if latent_only else 0
    n_tiles = L // tm - skip
    hm = pl.BlockSpec((nb, C_HEADS, tm, LANES), lambda b, i: (b, 0, i + skip, 0))
    tok = pl.BlockSpec((nb, tm, D_MODEL), lambda b, i: (b, i + skip, 0))
    row = pl.BlockSpec((1, D_MODEL), lambda b, i: (0, 0))
    return pl.pallas_call(
        _c_out_kernel,
        grid=(B // nb, n_tiles),
        in_specs=[
            hm, hm, tok, tok,
            pl.BlockSpec((nb, 1, 3, D_MODEL), lambda b, i: (b, (i + skip >= ctx_tiles).astype(jnp.int32), 0, 0)),
            pl.BlockSpec((1, LANES), lambda b, i: (0, 0)),
            pl.BlockSpec((D_MODEL, D_MODEL), lambda b, i: (0, 0)),
            row, row,
        ],
        out_specs=pl.BlockSpec((nb, tm, D_MODEL), lambda b, i: (b, i, 0)),
        out_shape=jax.ShapeDtypeStruct((B, n_tiles * tm, D_MODEL), F32),
        compiler_params=_params(2),
        name="c_out",
    )(o_f, o_b, sg, xcat, modsel, gnorm, w_out, ln_g, ln_b)


def _rope_tables(seq, ctx_len):
    t = jnp.arange(seq)
    freqs = ROPE_BASE ** (-jnp.arange(16, dtype=F32) / 16)
    ang_r = (t // GRID_W).astype(F32)[:, None] * freqs[None, :]
    ang_c = (t % GRID_W).astype(F32)[:, None] * freqs[None, :]
    cos = jnp.concatenate([jnp.cos(ang_r)] * 2 + [jnp.cos(ang_c)] * 2, axis=-1)
    sin = jnp.concatenate([-jnp.sin(ang_r), jnp.sin(ang_r), -jnp.sin(ang_c), jnp.sin(ang_c)], axis=-1)
    cos = jnp.concatenate([jnp.ones((ctx_len, 64), F32), cos], axis=0)
    sin = jnp.concatenate([jnp.zeros((ctx_len, 64), F32), sin], axis=0)
    return jnp.tile(cos, (1, 2)), jnp.tile(sin, (1, 2))


def kernel(x, c, ctx, c_ctx, w_ada, b_ada, ln_g, ln_b, w_in_ab, w_out_ab, sink_ab, conv_ab,
           w_in_c, w_out_c, lb_c, gnorm_c):
    B, seq, _ = x.shape
    ctx_len = ctx.shape[1]
    depth = w_ada.shape[0]
    assert seq % ROW_TILE == 0 and ctx_len % ROW_TILE == 0 and B <= 8

    s = jnp.concatenate([c, c_ctx[None, :], jnp.zeros((16 - B - 1, D_MODEL), F32)], axis=0)
    mod_all = _ada(s, w_ada, b_ada)
    mod_lat = mod_all[:, :B].reshape(depth, B, 1, 3, D_MODEL)
    mod_ctx = jnp.broadcast_to(mod_all[:, B].reshape(depth, 1, 1, 3, D_MODEL), mod_lat.shape)
    modsel = jnp.concatenate([mod_ctx, mod_lat], axis=2)

    lb_p = jax.nn.softmax(lb_c.astype(F32), axis=1)
    lb_all = jnp.cumsum(lb_p, axis=1) - lb_p[:, :1]

    cos_t, sin_t = _rope_tables(seq, ctx_len)
    xcat = None
    for l in range(depth):
        j = l // 2
        g_row, b_row = ln_g[l][None, :], ln_b[l][None, :]
        if l % 2 == 0:
            stream = (ctx, x, ctx_len) if xcat is None else (xcat, xcat, 0)
            q, k, v, sga, obg = _ab_in(stream, modsel[l], w_in_ab[j].astype(BF16), conv_ab[j], cos_t, sin_t, ctx_len)
            sink_b = jnp.broadcast_to(sink_ab[j].astype(F32)[:, None], (A_HEADS, LANES))
            xcat = _ab_attn(stream, modsel[l], q, k, v, sga, obg, sink_b, w_out_ab[j].astype(BF16),
                            g_row, b_row, ctx_len)
        else:
            q, ff, fb, v, sg = _c_in(xcat, modsel[l], w_in_c[j].astype(BF16), lb_all[:, j], ctx_len)
            o_f, o_b = _c_scan(q, ff, fb, v, ctx_len)
            xcat = _c_out(xcat, modsel[l], o_f, o_b, sg, gnorm_c[j][None, :], w_out_c[j].astype(BF16),
                          g_row, b_row, ctx_len, latent_only=(l == depth - 1))
    return xcat if depth % 2 == 0 else xcat[:, ctx_len:]
```
